```python
import math
import jax, jax.numpy as jnp
from jax import lax
import numpy as np

D_MODEL = 2048
BATCH = 4
SEQ = 4096
DEPTH = 4

CONV_CH = D_MODEL
CONV_WIDTH = 31
CONV_GROUPS = 16
SGU_CH = D_MODEL
SGU_GROUPS = 8
SGU_GROUP_CH = SGU_CH // SGU_GROUPS
CHUNK = 128
N_BRANCH = 2
FFN_HIDDEN = int(math.ceil(8 * D_MODEL / 3 / 256) * 256)
IN_WIDTH = 2 * CONV_CH + 2 * SGU_CH + N_BRANCH * D_MODEL
EPS = 1e-6

kernel_name = "hybrid_conformer_conv_gmlp_sandwich_block"


def rms_norm(x, g):
    xf = x.astype(jnp.float32)
    y = xf * lax.rsqrt(jnp.mean(xf * xf, axis=-1, keepdims=True) + EPS)
    return (y * g.astype(jnp.float32)).astype(x.dtype)


def layer_norm(x, g, b):
    xf = x.astype(jnp.float32)
    mu = jnp.mean(xf, axis=-1, keepdims=True)
    xc = xf - mu
    var = jnp.mean(xc * xc, axis=-1, keepdims=True)
    y = xc * lax.rsqrt(var + EPS)
    return (y * g.astype(jnp.float32) + b.astype(jnp.float32)).astype(x.dtype)


def conformer_conv_branch(a_in, a_gate, conv_w, conv_b, ln_g, ln_b, w_out):
    h = a_in * jax.nn.sigmoid(a_gate)
    rhs = conv_w[:, None, :].astype(h.dtype)
    h = lax.conv_general_dilated(
        h, rhs, window_strides=(1,), padding=[(CONV_WIDTH - 1, 0)],
        dimension_numbers=("NWC", "WIO", "NWC"),
        feature_group_count=CONV_CH)
    h = h + conv_b
    h = layer_norm(h, ln_g, ln_b)
    h = jax.nn.silu(h)
    return h @ w_out


def chunked_sgu_branch(b_in, ln_g, ln_b, w_spatial, b_spatial, w_out):
    bsz, seq, _ = b_in.shape
    z = jax.nn.gelu(b_in, approximate=False)
    u, v = jnp.split(z, 2, axis=-1)
    v = layer_norm(v, ln_g, ln_b)
    n_chunks = seq // CHUNK
    v = v.reshape(bsz, n_chunks, CHUNK, SGU_GROUPS, SGU_GROUP_CH)
    causal = jnp.tril(jnp.ones((CHUNK, CHUNK), dtype=bool))
    w_masked = jnp.where(causal[None], w_spatial, 0).astype(v.dtype)
    mixed = jnp.einsum("gts,bnsgc->bntgc", w_masked, v)
    mixed = mixed + jnp.transpose(b_spatial)[None, None, :, :, None].astype(v.dtype)
    mixed = mixed.reshape(bsz, seq, SGU_CH)
    return (u * mixed) @ w_out


def swiglu_ffn(h, w_gate_up, w_down):
    gu = h @ w_gate_up
    g, up = jnp.split(gu, 2, axis=-1)
    return (jax.nn.silu(g) * up) @ w_down


def setup_inputs(seed: int = 0) -> dict:
    key = jax.random.key(seed)
    ks = jax.random.split(key, 20)
    L = DEPTH
    f32 = jnp.float32

    def nrm(k, shape, scale):
        return jax.random.normal(k, shape, f32) * scale

    def gain(k, shape):
        return 1.0 + 0.05 * jax.random.normal(k, shape, f32)

    x = jax.random.normal(ks[0], (BATCH, SEQ, D_MODEL), f32)
    return {
        "x": x,
        "norm_mix_pre": gain(ks[1], (L, D_MODEL)),
        "norm_mix_post": gain(ks[2], (L, D_MODEL)),
        "norm_ffn_pre": gain(ks[3], (L, D_MODEL)),
        "norm_ffn_post": gain(ks[4], (L, D_MODEL)),
        "w_in": nrm(ks[5], (L, D_MODEL, IN_WIDTH), D_MODEL ** -0.5),
        "b_gate": nrm(ks[6], (L, N_BRANCH, D_MODEL), 0.01),
        "conv_w": nrm(ks[7], (L, CONV_WIDTH, CONV_CH), CONV_WIDTH ** -0.5),
        "conv_b": nrm(ks[8], (L, CONV_CH), 0.02),
        "conv_ln_g": gain(ks[9], (L, CONV_CH)),
        "conv_ln_b": nrm(ks[10], (L, CONV_CH), 0.02),
        "w_a_out": nrm(ks[11], (L, CONV_CH, D_MODEL), CONV_CH ** -0.5),
        "sgu_ln_g": gain(ks[12], (L, SGU_CH)),
        "sgu_ln_b": nrm(ks[13], (L, SGU_CH), 0.02),
        "w_spatial": nrm(ks[14], (L, SGU_GROUPS, CHUNK, CHUNK), CHUNK ** -0.5),
        "b_spatial": gain(ks[15], (L, SGU_GROUPS, CHUNK)),
        "w_b_out": nrm(ks[16], (L, SGU_CH, D_MODEL), SGU_CH ** -0.5),
        "w_o": nrm(ks[17], (L, D_MODEL, D_MODEL), D_MODEL ** -0.5),
        "w_gate_up": nrm(ks[18], (L, D_MODEL, 2 * FFN_HIDDEN), D_MODEL ** -0.5),
        "w_down": nrm(ks[19], (L, FFN_HIDDEN, D_MODEL), FFN_HIDDEN ** -0.5),
    }


def reference(x, norm_mix_pre, norm_mix_post, norm_ffn_pre, norm_ffn_post,
              w_in, b_gate, conv_w, conv_b, conv_ln_g, conv_ln_b, w_a_out,
              sgu_ln_g, sgu_ln_b, w_spatial, b_spatial, w_b_out, w_o,
              w_gate_up, w_down):
    bsz, seq, _ = x.shape
    split_pts = [CONV_CH, 2 * CONV_CH, 2 * CONV_CH + 2 * SGU_CH]
    for i in range(DEPTH):
        h = rms_norm(x, norm_mix_pre[i])
        proj = h @ w_in[i]
        a_in, a_gate, b_in, gate_logits = jnp.split(proj, split_pts, axis=-1)
        y_a = conformer_conv_branch(a_in, a_gate, conv_w[i], conv_b[i],
                                    conv_ln_g[i], conv_ln_b[i], w_a_out[i])
        y_b = chunked_sgu_branch(b_in, sgu_ln_g[i], sgu_ln_b[i], w_spatial[i],
                                 b_spatial[i], w_b_out[i])
        gates = jax.nn.sigmoid(gate_logits.reshape(bsz, seq, N_BRANCH, D_MODEL) + b_gate[i])
        merged = gates[:, :, 0, :] * y_a + gates[:, :, 1, :] * y_b
        x = x + rms_norm(merged @ w_o[i], norm_mix_post[i])
        h = rms_norm(x, norm_ffn_pre[i])
        x = x + rms_norm(swiglu_ffn(h, w_gate_up[i], w_down[i]), norm_ffn_post[i])
    return x
```

```python
import functools
import math

import jax
import jax.numpy as jnp
from jax import lax
from jax.experimental import pallas as pl
from jax.experimental.pallas import tpu as pltpu

F32 = jnp.float32
BF16 = jnp.bfloat16

EPS = 1e-6
CONV_WIDTH = 31
SGU_GROUPS = 8
CHUNK = 128
LANES = 128
SUBLANES = 8
HALO = 32
VMEM_LIMIT_BYTES = 56 * 1024 * 1024


def _params(n_axes):
    return pltpu.CompilerParams(
        dimension_semantics=("arbitrary",) * n_axes,
        vmem_limit_bytes=VMEM_LIMIT_BYTES)


def _rms(x, g):
    ms = jnp.mean(x * x, axis=-1, keepdims=True)
    return x * lax.rsqrt(ms + EPS) * g


def _layer_norm(x, g, b):
    mu = jnp.mean(x, axis=-1, keepdims=True)
    xc = x - mu
    var = jnp.mean(xc * xc, axis=-1, keepdims=True)
    return xc * lax.rsqrt(var + EPS) * g + b


def _sigmoid(x):
    return 1.0 / (1.0 + jnp.exp(-x))


def _rms_kernel(x_ref, g_ref, o_ref):
    o_ref[...] = _rms(x_ref[...], g_ref[...]).astype(o_ref.dtype)


def _rms_call(x, g, *, tm=512):
    n, d = x.shape
    return pl.pallas_call(
        _rms_kernel,
        grid=(n // tm,),
        in_specs=[pl.BlockSpec((tm, d), lambda i: (i, 0)),
                  pl.BlockSpec((1, d), lambda i: (0, 0))],
        out_specs=pl.BlockSpec((tm, d), lambda i: (i, 0)),
        out_shape=jax.ShapeDtypeStruct((n, d), BF16),
        compiler_params=_params(1),
        name="rms0",
    )(x, g)


def _glu_epilogue(a, g):
    return a * _sigmoid(g)


def _gelu_epilogue(x):
    return 0.5 * x * (1.0 + lax.erf(x * (1.0 / math.sqrt(2.0))))


def _gate_epilogue(x, bias):
    return _sigmoid(x + bias)


def _swiglu_epilogue(g, u):
    return g * _sigmoid(g) * u


def _proj_kernel(*refs, n_w, has_bias, epilogue):
    h_ref = refs[0]
    w_refs = refs[1:1 + n_w]
    o_ref = refs[-1]
    h = h_ref[...]
    accs = [jnp.dot(h, w[...], preferred_element_type=F32) for w in w_refs]
    if has_bias:
        accs.append(refs[1 + n_w][...])
    o_ref[...] = epilogue(*accs).astype(o_ref.dtype)


def _proj_call(h, w_stack, layer, col_offsets, n_out, epilogue, *, bias=None, tm, tn, name):
    n, k = h.shape
    in_specs = [pl.BlockSpec((tm, k), lambda i, j: (i, 0))]
    for off in col_offsets:
        in_specs.append(pl.BlockSpec(
            (None, k, tn), lambda i, j, off=off: (layer, 0, j + off // tn)))
    args = [h] + [w_stack] * len(col_offsets)
    if bias is not None:
        in_specs.append(pl.BlockSpec((1, tn), lambda i, j: (0, j)))
        args.append(bias)
    kern = functools.partial(_proj_kernel, n_w=len(col_offsets),
                             has_bias=bias is not None, epilogue=epilogue)
    return pl.pallas_call(
        kern,
        grid=(n // tm, n_out // tn),
        in_specs=in_specs,
        out_specs=pl.BlockSpec((tm, tn), lambda i, j: (i, j)),
        out_shape=jax.ShapeDtypeStruct((n, n_out), BF16),
        compiler_params=_params(2),
        name=name,
    )(*args)


def _conv_kernel(cur_ref, halo_ref, w_ref, cb_ref, lg_ref, lb_ref, o_ref, xs_ref, y_ref, *, t):
    c = cur_ref.shape[-1]
    s = pl.program_id(1)
    halo = halo_ref[...].astype(F32)
    xs_ref[0:HALO, :] = jnp.where(s > 0, halo, 0.0)
    xs_ref[HALO:HALO + t, :] = cur_ref[...].astype(F32)
    first = HALO - (CONV_WIDTH - 1)

    def lane_chunk(ci, carry):
        c0 = pl.multiple_of(ci * LANES, LANES)
        taps = [jnp.broadcast_to(w_ref[pl.ds(k, 1), pl.ds(c0, LANES)], (SUBLANES, LANES))
                for k in range(CONV_WIDTH)]
        for r in range(t // SUBLANES):
            acc = None
            for k in range(CONV_WIDTH):
                term = xs_ref[pl.ds(r * SUBLANES + first + k, SUBLANES), pl.ds(c0, LANES)] * taps[k]
                acc = term if acc is None else acc + term
            y_ref[pl.ds(r * SUBLANES, SUBLANES), pl.ds(c0, LANES)] = acc
        return carry

    lax.fori_loop(0, c // LANES, lane_chunk, 0)
    y = _layer_norm(y_ref[...] + cb_ref[...], lg_ref[...], lb_ref[...])
    o_ref[...] = (y * _sigmoid(y)).astype(o_ref.dtype)


def _conv_call(glu, conv_w, conv_b, ln_g, ln_b, *, t=256):
    b, s, c = glu.shape
    halo_blocks = t // HALO
    kern = functools.partial(_conv_kernel, t=t)
    vec = lambda: pl.BlockSpec((1, c), lambda bi, si: (0, 0))
    return pl.pallas_call(
        kern,
        grid=(b, s // t),
        in_specs=[
            pl.BlockSpec((None, t, c), lambda bi, si: (bi, si, 0)),
            pl.BlockSpec((None, HALO, c),
                         lambda bi, si: (bi, jnp.maximum(si * halo_blocks - 1, 0), 0)),
            pl.BlockSpec((CONV_WIDTH, c), lambda bi, si: (0, 0)),
            vec(), vec(), vec(),
        ],
        out_specs=pl.BlockSpec((None, t, c), lambda bi, si: (bi, si, 0)),
        out_shape=jax.ShapeDtypeStruct((b, s, c), BF16),
        scratch_shapes=[pltpu.VMEM((HALO + t, c), F32), pltpu.VMEM((t, c), F32)],
        compiler_params=_params(2),
        name="conv_ln_swish",
    )(glu, glu, conv_w, conv_b, ln_g, ln_b)


def _sgu_kernel(u_ref, v_ref, lg_ref, lb_ref, wsp_ref, bsp_ref, o_ref, *, t):
    c = u_ref.shape[-1]
    gc = c // SGU_GROUPS
    vn = _layer_norm(v_ref[...].astype(F32), lg_ref[...], lb_ref[...]).astype(BF16)
    row = lax.broadcasted_iota(jnp.int32, (CHUNK, CHUNK), 0)
    col = lax.broadcasted_iota(jnp.int32, (CHUNK, CHUNK), 1)
    causal = col <= row
    for g in range(SGU_GROUPS):
        w = jnp.where(causal, wsp_ref[g], 0.0).astype(BF16)
        bias = bsp_ref[g]
        for n in range(t // CHUNK):
            rows = slice(n * CHUNK, (n + 1) * CHUNK)
            cols = slice(g * gc, (g + 1) * gc)
            mixed = jnp.dot(w, vn[rows, cols], preferred_element_type=F32) + bias
            o_ref[rows, cols] = (u_ref[rows, cols].astype(F32) * mixed).astype(o_ref.dtype)


def _sgu_call(z, ln_g, ln_b, w_spatial, b_spatial_b, *, t=256):
    n, c2 = z.shape
    c = c2 // 2
    gc = c // SGU_GROUPS
    kern = functools.partial(_sgu_kernel, t=t)
    vec = lambda: pl.BlockSpec((1, c), lambda i: (0, 0))
    return pl.pallas_call(
        kern,
        grid=(n // t,),
        in_specs=[
            pl.BlockSpec((t, c), lambda i: (i, 0)),
            pl.BlockSpec((t, c), lambda i: (i, 1)),
            vec(), vec(),
            pl.BlockSpec((SGU_GROUPS, CHUNK, CHUNK), lambda i: (0, 0, 0)),
            pl.BlockSpec((SGU_GROUPS, CHUNK, gc), lambda i: (0, 0, 0)),
        ],
        out_specs=pl.BlockSpec((t, c), lambda i: (i, 0)),
        out_shape=jax.ShapeDtypeStruct((n, c), BF16),
        compiler_params=_params(1),
        name="sgu_mix",
    )(z, z, ln_g, ln_b, w_spatial, b_spatial_b)


def _residual_out(x_ref, o, gpost_ref, gnext_ref, xo_ref, ho_ref):
    xn = x_ref[...] + _rms(o, gpost_ref[...])
    xo_ref[...] = xn
    if ho_ref is not None:
        ho_ref[...] = _rms(xn, gnext_ref[...]).astype(ho_ref.dtype)


def _mix_out_kernel(a_ref, b_ref, g0_ref, g1_ref, x_ref, wa_ref, wb_ref, wo_ref,
                    gpost_ref, gnext_ref, xo_ref, ho_ref):
    ya = jnp.dot(a_ref[...], wa_ref[...], preferred_element_type=F32)
    yb = jnp.dot(b_ref[...], wb_ref[...], preferred_element_type=F32)
    merged = g0_ref[...].astype(F32) * ya + g1_ref[...].astype(F32) * yb
    o = jnp.dot(merged.astype(BF16), wo_ref[...], preferred_element_type=F32)
    _residual_out(x_ref, o, gpost_ref, gnext_ref, xo_ref, ho_ref)


def _ffn_out_kernel(act_ref, x_ref, wd_ref, gpost_ref, *rest):
    gnext_ref, xo_ref, ho_ref = rest if len(rest) == 3 else (None, rest[0], None)
    o = jnp.dot(act_ref[...], wd_ref[...], preferred_element_type=F32)
    _residual_out(x_ref, o, gpost_ref, gnext_ref, xo_ref, ho_ref)


def _resident(shape, index):
    return pl.BlockSpec(shape, index, pipeline_mode=pl.Buffered(1))


def _mix_out_call(a_act, b_act, gates, x, wa, wb, wo, layer, gpost, gnext, *, tm=256):
    n, d = x.shape
    row = lambda: pl.BlockSpec((tm, d), lambda i: (i, 0))
    vec = lambda: pl.BlockSpec((1, d), lambda i: (0, 0))
    wspec = lambda w: _resident((None,) + w.shape[1:], lambda i: (layer, 0, 0))
    return pl.pallas_call(
        _mix_out_kernel,
        grid=(n // tm,),
        in_specs=[row(), row(), row(), pl.BlockSpec((tm, d), lambda i: (i, 1)), row(),
                  wspec(wa), wspec(wb), wspec(wo), vec(), vec()],
        out_specs=[row(), row()],
        out_shape=[jax.ShapeDtypeStruct((n, d), F32), jax.ShapeDtypeStruct((n, d), BF16)],
        compiler_params=_params(1),
        name="mix_out",
    )(a_act, b_act, gates, gates, x, wa, wb, wo, gpost, gnext)


def _ffn_out_call(act, x, wd, layer, gpost, gnext, *, tm=256):
    n, d = x.shape
    f = act.shape[1]
    row = lambda: pl.BlockSpec((tm, d), lambda i: (i, 0))
    vec = lambda: pl.BlockSpec((1, d), lambda i: (0, 0))
    in_specs = [pl.BlockSpec((tm, f), lambda i: (i, 0)), row(),
                _resident((None, f, d), lambda i: (layer, 0, 0)), vec()]
    args = [act, x, wd, gpost]
    out_specs = [row()]
    out_shape = [jax.ShapeDtypeStruct((n, d), F32)]
    if gnext is not None:
        in_specs.append(vec())
        args.append(gnext)
        out_specs.append(row())
        out_shape.append(jax.ShapeDtypeStruct((n, d), BF16))
    outs = pl.pallas_call(
        _ffn_out_kernel,
        grid=(n // tm,),
        in_specs=in_specs,
        out_specs=out_specs,
        out_shape=out_shape,
        compiler_params=_params(1),
        name="ffn_out",
    )(*args)
    return (outs[0], outs[1]) if gnext is not None else (outs[0], None)


def kernel(x, norm_mix_pre, norm_mix_post, norm_ffn_pre, norm_ffn_post, w_in, b_gate, conv_w,
           conv_b, conv_ln_g, conv_ln_b, w_a_out, sgu_ln_g, sgu_ln_b, w_spatial, b_spatial,
           w_b_out, w_o, w_gate_up, w_down):
    bsz, seq, d = x.shape
    depth = w_in.shape[0]
    n = bsz * seq
    ffn = w_down.shape[1]
    gc = d // SGU_GROUPS

    w_in_b = w_in.astype(BF16)
    w_a_b = w_a_out.astype(BF16)
    w_b_b = w_b_out.astype(BF16)
    w_o_b = w_o.astype(BF16)
    w_gu_b = w_gate_up.astype(BF16)
    w_d_b = w_down.astype(BF16)

    xf = x.reshape(n, d)
    h = _rms_call(xf, norm_mix_pre[0][None])
    for l in range(depth):
        glu = _proj_call(h, w_in_b, l, (0, d), d, _glu_epilogue, tm=1024, tn=512, name="proj_glu")
        z = _proj_call(h, w_in_b, l, (2 * d,), 2 * d, _gelu_epilogue, tm=1024, tn=1024,
                       name="proj_gelu")
        gates = _proj_call(h, w_in_b, l, (4 * d,), 2 * d, _gate_epilogue,
                           bias=b_gate[l].reshape(1, 2 * d), tm=1024, tn=1024, name="proj_gate")
        a_act = _conv_call(glu.reshape(bsz, seq, d), conv_w[l], conv_b[l][None],
                           conv_ln_g[l][None], conv_ln_b[l][None]).reshape(n, d)
        bsp = jnp.broadcast_to(b_spatial[l][:, :, None], (SGU_GROUPS, CHUNK, gc))
        b_act = _sgu_call(z, sgu_ln_g[l][None], sgu_ln_b[l][None], w_spatial[l], bsp)
        xf, h = _mix_out_call(a_act, b_act, gates, xf, w_a_b, w_b_b, w_o_b, l,
                              norm_mix_post[l][None], norm_ffn_pre[l][None])
        act = _proj_call(h, w_gu_b, l, (0, ffn), ffn, _swiglu_epilogue, tm=1024, tn=512,
                         name="proj_swiglu")
        gnext = norm_mix_pre[l + 1][None] if l + 1 < depth else None
        xf, h = _ffn_out_call(act, xf, w_d_b, l, norm_ffn_post[l][None], gnext)
    return xf.reshape(bsz, seq, d)
```

```python
import functools
import math

import jax
import jax.numpy as jnp
from jax import lax
from jax.experimental import pallas as pl
from jax.experimental.pallas import tpu as pltpu

F32 = jnp.float32
BF16 = jnp.bfloat16

EPS = 1e-6
CONV_WIDTH = 31
SGU_GROUPS = 8
CHUNK = 128
LANES = 128
SUBLANES = 8
HALO = 32
VMEM_LIMIT_BYTES = 56 * 1024 * 1024


def _params(n_axes):
    return pltpu.CompilerParams(
        dimension_semantics=("arbitrary",) * n_axes,
        vmem_limit_bytes=VMEM_LIMIT_BYTES)


def _rms(x, g):
    ms = jnp.mean(x * x, axis=-1, keepdims=True)
    return x * lax.rsqrt(ms + EPS) * g


def _layer_norm(x, g, b):
    mu = jnp.mean(x, axis=-1, keepdims=True)
    xc = x - mu
    var = jnp.mean(xc * xc, axis=-1, keepdims=True)
    return xc * lax.rsqrt(var + EPS) * g + b


def _sigmoid(x):
    return 1.0 / (1.0 + jnp.exp(-x))


def _rms_kernel(x_ref, g_ref, o_ref):
    o_ref[...] = _rms(x_ref[...], g_ref[...]).astype(o_ref.dtype)


def _rms_call(x, g, *, tm=512):
    n, d = x.shape
    return pl.pallas_call(
        _rms_kernel,
        grid=(n // tm,),
        in_specs=[pl.BlockSpec((tm, d), lambda i: (i, 0)),
                  pl.BlockSpec((1, d), lambda i: (0, 0))],
        out_specs=pl.BlockSpec((tm, d), lambda i: (i, 0)),
        out_shape=jax.ShapeDtypeStruct((n, d), BF16),
        compiler_params=_params(1),
        name="rms0",
    )(x, g)


def _glu_epilogue(a, g):
    return a * _sigmoid(g)


def _gelu_epilogue(x):
    return 0.5 * x * (1.0 + lax.erf(x * (1.0 / math.sqrt(2.0))))


def _gate_epilogue(x, bias):
    return _sigmoid(x + bias)


def _swiglu_epilogue(g, u):
    return g * _sigmoid(g) * u


def _proj_kernel(*refs, n_w, has_bias, epilogue):
    h_ref = refs[0]
    w_refs = refs[1:1 + n_w]
    o_ref = refs[1 + n_w + int(has_bias)]
    wb_refs = refs[2 + n_w + int(has_bias):]

    @pl.when(pl.program_id(1) == 0)
    def _():
        for w, wb in zip(w_refs, wb_refs, strict=True):
            wb[...] = w[...].astype(BF16)

    h = h_ref[...]
    accs = [jnp.dot(h, wb[...], preferred_element_type=F32) for wb in wb_refs]
    if has_bias:
        accs.append(refs[1 + n_w][...])
    o_ref[...] = epilogue(*accs).astype(o_ref.dtype)


def _proj_call(h, w_stack, layer, col_offsets, n_out, epilogue, *, bias=None, tm, tn, name):
    n, k = h.shape
    in_specs = [pl.BlockSpec((tm, k), lambda j, i: (i, 0))]
    for off in col_offsets:
        in_specs.append(pl.BlockSpec(
            (None, k, tn), lambda j, i, off=off: (layer, 0, j + off // tn)))
    args = [h] + [w_stack] * len(col_offsets)
    if bias is not None:
        in_specs.append(pl.BlockSpec((1, tn), lambda j, i: (0, j)))
        args.append(bias)
    kern = functools.partial(_proj_kernel, n_w=len(col_offsets),
                             has_bias=bias is not None, epilogue=epilogue)
    return pl.pallas_call(
        kern,
        grid=(n_out // tn, n // tm),
        in_specs=in_specs,
        out_specs=pl.BlockSpec((tm, tn), lambda j, i: (i, j)),
        out_shape=jax.ShapeDtypeStruct((n, n_out), BF16),
        scratch_shapes=[pltpu.VMEM((k, tn), BF16) for _ in col_offsets],
        compiler_params=_params(2),
        name=name,
    )(*args)


def _conv_kernel(cur_ref, halo_ref, w_ref, cb_ref, lg_ref, lb_ref, o_ref, xs_ref, y_ref, *, t):
    c = cur_ref.shape[-1]
    s = pl.program_id(1)
    halo = halo_ref[...].astype(F32)
    xs_ref[0:HALO, :] = jnp.where(s > 0, halo, 0.0)
    xs_ref[HALO:HALO + t, :] = cur_ref[...].astype(F32)
    first = HALO - (CONV_WIDTH - 1)
    last = first + CONV_WIDTH - 1
    n_blocks = t // SUBLANES
    sub = lax.broadcasted_iota(jnp.int32, (SUBLANES, LANES), 0)

    def lane_chunk(ci, carry):
        c0 = pl.multiple_of(ci * LANES, LANES)
        taps = {o: jnp.broadcast_to(w_ref[pl.ds(o - first, 1), pl.ds(c0, LANES)], (SUBLANES, LANES))
                for o in range(first, last + 1)}
        blocks = {}

        def xs_block(j):
            if j not in blocks:
                blocks[j] = xs_ref[pl.ds(j * SUBLANES, SUBLANES), pl.ds(c0, LANES)]
            return blocks[j]

        def phase_sum(b, j):
            acc = None
            for a in range((last - b) // SUBLANES + 1):
                o = SUBLANES * a + b
                if o < first:
                    continue
                term = xs_block(j + a) * taps[o]
                acc = term if acc is None else acc + term
            return acc

        prev = [phase_sum(b, 0) for b in range(1, SUBLANES)]
        for r in range(n_blocks):
            nxt = [phase_sum(b, r + 1) for b in range(1, SUBLANES)]
            out = phase_sum(0, r)
            for b in range(1, SUBLANES):
                merged = jnp.where(sub >= b, prev[b - 1], nxt[b - 1])
                out = out + pltpu.roll(merged, SUBLANES - b, axis=0)
            y_ref[pl.ds(r * SUBLANES, SUBLANES), pl.ds(c0, LANES)] = out
            blocks.pop(r, None)
            prev = nxt
        return carry

    lax.fori_loop(0, c // LANES, lane_chunk, 0)
    y = _layer_norm(y_ref[...] + cb_ref[...], lg_ref[...], lb_ref[...])
    o_ref[...] = (y * _sigmoid(y)).astype(o_ref.dtype)


def _conv_call(glu, conv_w, conv_b, ln_g, ln_b, *, t=256):
    b, s, c = glu.shape
    halo_blocks = t // HALO
    kern = functools.partial(_conv_kernel, t=t)
    vec = lambda: pl.BlockSpec((1, c), lambda bi, si: (0, 0))
    return pl.pallas_call(
        kern,
        grid=(b, s // t),
        in_specs=[
            pl.BlockSpec((None, t, c), lambda bi, si: (bi, si, 0)),
            pl.BlockSpec((None, HALO, c),
                         lambda bi, si: (bi, jnp.maximum(si * halo_blocks - 1, 0), 0)),
            pl.BlockSpec((CONV_WIDTH, c), lambda bi, si: (0, 0)),
            vec(), vec(), vec(),
        ],
        out_specs=pl.BlockSpec((None, t, c), lambda bi, si: (bi, si, 0)),
        out_shape=jax.ShapeDtypeStruct((b, s, c), BF16),
        scratch_shapes=[pltpu.VMEM((HALO + t, c), F32), pltpu.VMEM((t, c), F32)],
        compiler_params=_params(2),
        name="conv_ln_swish",
    )(glu, glu, conv_w, conv_b, ln_g, ln_b)


def _sgu_kernel(u_ref, v_ref, lg_ref, lb_ref, wsp_ref, bsp_ref, o_ref, *, t):
    c = u_ref.shape[-1]
    gc = c // SGU_GROUPS
    vn = _layer_norm(v_ref[...].astype(F32), lg_ref[...], lb_ref[...]).astype(BF16)
    row = lax.broadcasted_iota(jnp.int32, (CHUNK, CHUNK), 0)
    col = lax.broadcasted_iota(jnp.int32, (CHUNK, CHUNK), 1)
    causal = col <= row
    for g in range(SGU_GROUPS):
        w = jnp.where(causal, wsp_ref[g], 0.0).astype(BF16)
        bias = bsp_ref[g]
        for n in range(t // CHUNK):
            rows = slice(n * CHUNK, (n + 1) * CHUNK)
            cols = slice(g * gc, (g + 1) * gc)
            mixed = jnp.dot(w, vn[rows, cols], preferred_element_type=F32) + bias
            o_ref[rows, cols] = (u_ref[rows, cols].astype(F32) * mixed).astype(o_ref.dtype)


def _sgu_call(z, ln_g, ln_b, w_spatial, b_spatial_b, *, t=256):
    n, c2 = z.shape
    c = c2 // 2
    gc = c // SGU_GROUPS
    kern = functools.partial(_sgu_kernel, t=t)
    vec = lambda: pl.BlockSpec((1, c), lambda i: (0, 0))
    return pl.pallas_call(
        kern,
        grid=(n // t,),
        in_specs=[
            pl.BlockSpec((t, c), lambda i: (i, 0)),
            pl.BlockSpec((t, c), lambda i: (i, 1)),
            vec(), vec(),
            pl.BlockSpec((SGU_GROUPS, CHUNK, CHUNK), lambda i: (0, 0, 0)),
            pl.BlockSpec((SGU_GROUPS, CHUNK, gc), lambda i: (0, 0, 0)),
        ],
        out_specs=pl.BlockSpec((t, c), lambda i: (i, 0)),
        out_shape=jax.ShapeDtypeStruct((n, c), BF16),
        compiler_params=_params(1),
        name="sgu_mix",
    )(z, z, ln_g, ln_b, w_spatial, b_spatial_b)


def _residual_out(x_ref, o, gpost_ref, gnext_ref, xo_ref, ho_ref):
    xn = x_ref[...] + _rms(o, gpost_ref[...])
    xo_ref[...] = xn
    if ho_ref is not None:
        ho_ref[...] = _rms(xn, gnext_ref[...]).astype(ho_ref.dtype)


def _mix_out_kernel(a_ref, b_ref, g0_ref, g1_ref, x_ref, wa_ref, wb_ref, wo_ref,
                    gpost_ref, gnext_ref, xo_ref, ho_ref):
    ya = jnp.dot(a_ref[...], wa_ref[...], preferred_element_type=F32)
    yb = jnp.dot(b_ref[...], wb_ref[...], preferred_element_type=F32)
    merged = g0_ref[...].astype(F32) * ya + g1_ref[...].astype(F32) * yb
    o = jnp.dot(merged.astype(BF16), wo_ref[...], preferred_element_type=F32)
    _residual_out(x_ref, o, gpost_ref, gnext_ref, xo_ref, ho_ref)


def _ffn_out_kernel(act_ref, x_ref, wd_ref, gpost_ref, *rest):
    gnext_ref, xo_ref, ho_ref = rest if len(rest) == 3 else (None, rest[0], None)
    o = jnp.dot(act_ref[...], wd_ref[...], preferred_element_type=F32)
    _residual_out(x_ref, o, gpost_ref, gnext_ref, xo_ref, ho_ref)


def _resident(shape, index):
    return pl.BlockSpec(shape, index, pipeline_mode=pl.Buffered(1))


def _mix_out_call(a_act, b_act, gates, x, wa, wb, wo, layer, gpost, gnext, *, tm=256):
    n, d = x.shape
    row = lambda: pl.BlockSpec((tm, d), lambda i: (i, 0))
    vec = lambda: pl.BlockSpec((1, d), lambda i: (0, 0))
    wspec = lambda w: _resident((None,) + w.shape[1:], lambda i: (layer, 0, 0))
    return pl.pallas_call(
        _mix_out_kernel,
        grid=(n // tm,),
        in_specs=[row(), row(), row(), pl.BlockSpec((tm, d), lambda i: (i, 1)), row(),
                  wspec(wa), wspec(wb), wspec(wo), vec(), vec()],
        out_specs=[row(), row()],
        out_shape=[jax.ShapeDtypeStruct((n, d), F32), jax.ShapeDtypeStruct((n, d), BF16)],
        compiler_params=_params(1),
        name="mix_out",
    )(a_act, b_act, gates, gates, x, wa, wb, wo, gpost, gnext)


def _ffn_out_call(act, x, wd, layer, gpost, gnext, *, tm=256):
    n, d = x.shape
    f = act.shape[1]
    row = lambda: pl.BlockSpec((tm, d), lambda i: (i, 0))
    vec = lambda: pl.BlockSpec((1, d), lambda i: (0, 0))
    in_specs = [pl.BlockSpec((tm, f), lambda i: (i, 0)), row(),
                _resident((None, f, d), lambda i: (layer, 0, 0)), vec()]
    args = [act, x, wd, gpost]
    out_specs = [row()]
    out_shape = [jax.ShapeDtypeStruct((n, d), F32)]
    if gnext is not None:
        in_specs.append(vec())
        args.append(gnext)
        out_specs.append(row())
        out_shape.append(jax.ShapeDtypeStruct((n, d), BF16))
    outs = pl.pallas_call(
        _ffn_out_kernel,
        grid=(n // tm,),
        in_specs=in_specs,
        out_specs=out_specs,
        out_shape=out_shape,
        compiler_params=_params(1),
        name="ffn_out",
    )(*args)
    return (outs[0], outs[1]) if gnext is not None else (outs[0], None)


def kernel(x, norm_mix_pre, norm_mix_post, norm_ffn_pre, norm_ffn_post, w_in, b_gate, conv_w,
           conv_b, conv_ln_g, conv_ln_b, w_a_out, sgu_ln_g, sgu_ln_b, w_spatial, b_spatial,
           w_b_out, w_o, w_gate_up, w_down):
    bsz, seq, d = x.shape
    depth = w_in.shape[0]
    n = bsz * seq
    ffn = w_down.shape[1]
    gc = d // SGU_GROUPS

    w_a_b = w_a_out.astype(BF16)
    w_b_b = w_b_out.astype(BF16)
    w_o_b = w_o.astype(BF16)
    w_d_b = w_down.astype(BF16)

    xf = x.reshape(n, d)
    h = _rms_call(xf, norm_mix_pre[0][None])
    for l in range(depth):
        glu = _proj_call(h, w_in, l, (0, d), d, _glu_epilogue, tm=1024, tn=512, name="proj_glu")
        z = _proj_call(h, w_in, l, (2 * d,), 2 * d, _gelu_epilogue, tm=1024, tn=1024,
                       name="proj_gelu")
        gates = _proj_call(h, w_in, l, (4 * d,), 2 * d, _gate_epilogue,
                           bias=b_gate[l].reshape(1, 2 * d), tm=1024, tn=1024, name="proj_gate")
        a_act = _conv_call(glu.reshape(bsz, seq, d), conv_w[l], conv_b[l][None],
                           conv_ln_g[l][None], conv_ln_b[l][None]).reshape(n, d)
        bsp = jnp.broadcast_to(b_spatial[l][:, :, None], (SGU_GROUPS, CHUNK, gc))
        b_act = _sgu_call(z, sgu_ln_g[l][None], sgu_ln_b[l][None], w_spatial[l], bsp)
        xf, h = _mix_out_call(a_act, b_act, gates, xf, w_a_b, w_b_b, w_o_b, l,
                              norm_mix_post[l][None], norm_ffn_pre[l][None])
        act = _proj_call(h, w_gate_up, l, (0, ffn), ffn, _swiglu_epilogue, tm=1024, tn=512,
                         name="proj_swiglu")
        gnext = norm_mix_pre[l + 1][None] if l + 1 < depth else None
        xf, h = _ffn_out_call(act, xf, w_d_b, l, norm_ffn_post[l][None], gnext)
    return xf.reshape(bsz, seq, d)
```

```python
import functools
import math

import jax
import jax.numpy as jnp
from jax import lax
from jax.experimental import pallas as pl
from jax.experimental.pallas import tpu as pltpu

F32 = jnp.float32
BF16 = jnp.bfloat16

EPS = 1e-6
CONV_WIDTH = 31
SGU_GROUPS = 8
CHUNK = 128
LANES = 128
SUBLANES = 8
BF16_ROWS = 16
PIECE = 256
HALO = 32
VMEM_LIMIT_BYTES = 56 * 1024 * 1024


def _params(n_axes):
    return pltpu.CompilerParams(
        dimension_semantics=("arbitrary",) * n_axes,
        vmem_limit_bytes=VMEM_LIMIT_BYTES)


def _rms(x, g):
    ms = jnp.mean(x * x, axis=-1, keepdims=True)
    return x * lax.rsqrt(ms + EPS) * g


def _layer_norm(x, g, b):
    mu = jnp.mean(x, axis=-1, keepdims=True)
    xc = x - mu
    var = jnp.mean(xc * xc, axis=-1, keepdims=True)
    return xc * lax.rsqrt(var + EPS) * g + b


def _sigmoid(x):
    return 1.0 / (1.0 + jnp.exp(-x))


def _rms_kernel(x_ref, g_ref, o_ref):
    o_ref[...] = _rms(x_ref[...], g_ref[...]).astype(o_ref.dtype)


def _rms_call(x, g, *, tm=512):
    n, d = x.shape
    return pl.pallas_call(
        _rms_kernel,
        grid=(n // tm,),
        in_specs=[pl.BlockSpec((tm, d), lambda i: (i, 0)),
                  pl.BlockSpec((1, d), lambda i: (0, 0))],
        out_specs=pl.BlockSpec((tm, d), lambda i: (i, 0)),
        out_shape=jax.ShapeDtypeStruct((n, d), BF16),
        compiler_params=_params(1),
        name="rms0",
    )(x, g)


def _glu_epilogue(a, g):
    return a * _sigmoid(g)


def _gelu_epilogue(x):
    return 0.5 * x * (1.0 + lax.erf(x * (1.0 / math.sqrt(2.0))))


def _gate_epilogue(x, bias):
    return _sigmoid(x + bias)


def _swiglu_epilogue(g, u):
    return g * _sigmoid(g) * u


def _proj_kernel(*refs, n_w, has_bias, epilogue):
    h_ref = refs[0]
    w_refs = refs[1:1 + n_w]
    o_ref = refs[1 + n_w + int(has_bias)]
    wb_ref = refs[2 + n_w + int(has_bias)]
    tn = o_ref.shape[1]
    n_pieces = tn // PIECE

    @pl.when(pl.program_id(1) == 0)
    def _():
        for p in range(n_pieces):
            for q, w in enumerate(w_refs):
                dst = (p * n_w + q) * PIECE
                wb_ref[:, dst:dst + PIECE] = w[:, p * PIECE:(p + 1) * PIECE].astype(BF16)

    acc = jnp.dot(h_ref[...], wb_ref[...], preferred_element_type=F32)
    if n_w == 1:
        args = [acc] + ([refs[1 + n_w][...]] if has_bias else [])
        o_ref[...] = epilogue(*args).astype(o_ref.dtype)
        return
    for p in range(n_pieces):
        cols = slice(p * PIECE, (p + 1) * PIECE)
        args = [acc[:, (p * n_w + q) * PIECE:(p * n_w + q + 1) * PIECE] for q in range(n_w)]
        if has_bias:
            args.append(refs[1 + n_w][:, cols])
        o_ref[:, cols] = epilogue(*args).astype(o_ref.dtype)


def _proj_call(h, w_stack, layer, col_offsets, n_out, epilogue, *, bias=None, tm, tn, name):
    n, k = h.shape
    in_specs = [pl.BlockSpec((tm, k), lambda j, i: (i, 0))]
    for off in col_offsets:
        in_specs.append(pl.BlockSpec(
            (None, k, tn), lambda j, i, off=off: (layer, 0, j + off // tn)))
    args = [h] + [w_stack] * len(col_offsets)
    if bias is not None:
        in_specs.append(pl.BlockSpec((1, tn), lambda j, i: (0, j)))
        args.append(bias)
    kern = functools.partial(_proj_kernel, n_w=len(col_offsets),
                             has_bias=bias is not None, epilogue=epilogue)
    return pl.pallas_call(
        kern,
        grid=(n_out // tn, n // tm),
        in_specs=in_specs,
        out_specs=pl.BlockSpec((tm, tn), lambda j, i: (i, j)),
        out_shape=jax.ShapeDtypeStruct((n, n_out), BF16),
        scratch_shapes=[pltpu.VMEM((k, tn * len(col_offsets)), BF16)],
        compiler_params=_params(2),
        name=name,
    )(*args)


def _conv_tile(xs_ref, w_ref, y_ref, t):
    c = y_ref.shape[-1]
    first = HALO - (CONV_WIDTH - 1)
    last = first + CONV_WIDTH - 1
    n_blocks = t // SUBLANES
    sub = lax.broadcasted_iota(jnp.int32, (SUBLANES, LANES), 0)

    def lane_chunk(ci, carry):
        c0 = pl.multiple_of(ci * LANES, LANES)
        taps = {o: jnp.broadcast_to(w_ref[pl.ds(o - first, 1), pl.ds(c0, LANES)], (SUBLANES, LANES))
                for o in range(first, last + 1)}
        blocks = {}

        def xs_block(j):
            if j not in blocks:
                blocks[j] = xs_ref[pl.ds(j * SUBLANES, SUBLANES), pl.ds(c0, LANES)]
            return blocks[j]

        def phase_sum(b, j):
            acc = None
            for a in range((last - b) // SUBLANES + 1):
                o = SUBLANES * a + b
                if o < first:
                    continue
                term = xs_block(j + a) * taps[o]
                acc = term if acc is None else acc + term
            return acc

        prev = [phase_sum(b, 0) for b in range(1, SUBLANES)]
        for r in range(n_blocks):
            nxt = [phase_sum(b, r + 1) for b in range(1, SUBLANES)]
            out = phase_sum(0, r)
            for b in range(1, SUBLANES):
                merged = jnp.where(sub >= b, prev[b - 1], nxt[b - 1])
                out = out + pltpu.roll(merged, SUBLANES - b, axis=0)
            y_ref[pl.ds(r * SUBLANES, SUBLANES), pl.ds(c0, LANES)] = out
            blocks.pop(r, None)
            prev = nxt
        return carry

    lax.fori_loop(0, c // LANES, lane_chunk, 0)


def _branch_act_kernel(cur_ref, halo_ref, cw_ref, cb_ref, clg_ref, clb_ref, u_ref, v_ref, lg_ref,
                       lb_ref, wsp_ref, bsp_ref, a_ref, b_ref, xs_ref, y_ref, vn_ref, *, t):
    c = cur_ref.shape[-1]
    gc = c // SGU_GROUPS
    s = pl.program_id(1)
    xs_ref[0:HALO, :] = jnp.where(s > 0, halo_ref[...].astype(F32), 0.0)
    xs_ref[HALO:HALO + t, :] = cur_ref[...].astype(F32)
    _conv_tile(xs_ref, cw_ref, y_ref, t)

    cb, clg, clb = cb_ref[...], clg_ref[...], clb_ref[...]
    lg, lb = lg_ref[...], lb_ref[...]
    for r in range(0, t, BF16_ROWS):
        rows = pl.ds(r, BF16_ROWS)
        y = _layer_norm(y_ref[rows, :] + cb, clg, clb)
        a_ref[rows, :] = (y * _sigmoid(y)).astype(a_ref.dtype)
        vn_ref[rows, :] = _layer_norm(v_ref[rows, :].astype(F32), lg, lb).astype(BF16)

    row = lax.broadcasted_iota(jnp.int32, (CHUNK, CHUNK), 0)
    col = lax.broadcasted_iota(jnp.int32, (CHUNK, CHUNK), 1)
    causal = col <= row
    for g in range(SGU_GROUPS):
        w = jnp.where(causal, wsp_ref[g], 0.0).astype(BF16)
        bias = bsp_ref[g]
        for n in range(t // CHUNK):
            rows = slice(n * CHUNK, (n + 1) * CHUNK)
            cols = slice(g * gc, (g + 1) * gc)
            mixed = jnp.dot(w, vn_ref[rows, cols], preferred_element_type=F32) + bias
            b_ref[rows, cols] = (u_ref[rows, cols].astype(F32) * mixed).astype(b_ref.dtype)


def _branch_act_call(glu, conv_w, conv_b, conv_ln_g, conv_ln_b, z, ln_g, ln_b, w_spatial,
                     b_spatial_b, *, t=256):
    bsz, seq, c = glu.shape
    gc = c // SGU_GROUPS
    halo_blocks = t // HALO
    vec = lambda: pl.BlockSpec((1, c), lambda bi, si: (0, 0))
    tile = lambda col: pl.BlockSpec((None, t, c), lambda bi, si: (bi, si, col))
    return pl.pallas_call(
        functools.partial(_branch_act_kernel, t=t),
        grid=(bsz, seq // t),
        in_specs=[
            tile(0),
            pl.BlockSpec((None, HALO, c),
                         lambda bi, si: (bi, jnp.maximum(si * halo_blocks - 1, 0), 0)),
            pl.BlockSpec((CONV_WIDTH, c), lambda bi, si: (0, 0)),
            vec(), vec(), vec(),
            tile(0), tile(1), vec(), vec(),
            pl.BlockSpec((SGU_GROUPS, CHUNK, CHUNK), lambda bi, si: (0, 0, 0)),
            pl.BlockSpec((SGU_GROUPS, CHUNK, gc), lambda bi, si: (0, 0, 0)),
        ],
        out_specs=[tile(0), tile(0)],
        out_shape=[jax.ShapeDtypeStruct((bsz, seq, c), BF16),
                   jax.ShapeDtypeStruct((bsz, seq, c), BF16)],
        scratch_shapes=[pltpu.VMEM((HALO + t, c), F32), pltpu.VMEM((t, c), F32),
                        pltpu.VMEM((t, c), BF16)],
        compiler_params=_params(2),
        name="branch_act",
    )(glu, glu, conv_w, conv_b, conv_ln_g, conv_ln_b, z, z, ln_g, ln_b, w_spatial, b_spatial_b)


def _residual_out(x_ref, o_ref, gpost_ref, gnext_ref, xo_ref, ho_ref):
    gpost = gpost_ref[...]
    gnext = None if ho_ref is None else gnext_ref[...]
    for r in range(0, x_ref.shape[0], BF16_ROWS):
        rows = pl.ds(r, BF16_ROWS)
        xn = x_ref[rows, :] + _rms(o_ref[rows, :], gpost)
        xo_ref[rows, :] = xn
        if ho_ref is not None:
            ho_ref[rows, :] = _rms(xn, gnext).astype(ho_ref.dtype)


def _mix_out_kernel(a_ref, b_ref, g0_ref, g1_ref, x_ref, wa_ref, wb_ref, wo_ref,
                    gpost_ref, gnext_ref, xo_ref, ho_ref, o_scr):
    ya = jnp.dot(a_ref[...], wa_ref[...], preferred_element_type=F32)
    yb = jnp.dot(b_ref[...], wb_ref[...], preferred_element_type=F32)
    merged = g0_ref[...].astype(F32) * ya + g1_ref[...].astype(F32) * yb
    o_scr[...] = jnp.dot(merged.astype(BF16), wo_ref[...], preferred_element_type=F32)
    _residual_out(x_ref, o_scr, gpost_ref, gnext_ref, xo_ref, ho_ref)


def _ffn_out_kernel(act_ref, x_ref, wd_ref, gpost_ref, *rest):
    gnext_ref, xo_ref, ho_ref, o_scr = rest if len(rest) == 4 else (None, rest[0], None, rest[1])
    o_scr[...] = jnp.dot(act_ref[...], wd_ref[...], preferred_element_type=F32)
    _residual_out(x_ref, o_scr, gpost_ref, gnext_ref, xo_ref, ho_ref)


def _resident(shape, index):
    return pl.BlockSpec(shape, index, pipeline_mode=pl.Buffered(1))


def _mix_out_call(a_act, b_act, gates, x, wa, wb, wo, layer, gpost, gnext, *, tm=256):
    n, d = x.shape
    row = lambda: pl.BlockSpec((tm, d), lambda i: (i, 0))
    vec = lambda: pl.BlockSpec((1, d), lambda i: (0, 0))
    wspec = lambda w: _resident((None,) + w.shape[1:], lambda i: (layer, 0, 0))
    return pl.pallas_call(
        _mix_out_kernel,
        grid=(n // tm,),
        in_specs=[row(), row(), row(), pl.BlockSpec((tm, d), lambda i: (i, 1)), row(),
                  wspec(wa), wspec(wb), wspec(wo), vec(), vec()],
        out_specs=[row(), row()],
        out_shape=[jax.ShapeDtypeStruct((n, d), F32), jax.ShapeDtypeStruct((n, d), BF16)],
        scratch_shapes=[pltpu.VMEM((tm, d), F32)],
        compiler_params=_params(1),
        name="mix_out",
    )(a_act, b_act, gates, gates, x, wa, wb, wo, gpost, gnext)


def _ffn_out_call(act, x, wd, layer, gpost, gnext, *, tm=256):
    n, d = x.shape
    f = act.shape[1]
    row = lambda: pl.BlockSpec((tm, d), lambda i: (i, 0))
    vec = lambda: pl.BlockSpec((1, d), lambda i: (0, 0))
    in_specs = [pl.BlockSpec((tm, f), lambda i: (i, 0)), row(),
                _resident((None, f, d), lambda i: (layer, 0, 0)), vec()]
    args = [act, x, wd, gpost]
    out_specs = [row()]
    out_shape = [jax.ShapeDtypeStruct((n, d), F32)]
    if gnext is not None:
        in_specs.append(vec())
        args.append(gnext)
        out_specs.append(row())
        out_shape.append(jax.ShapeDtypeStruct((n, d), BF16))
    outs = pl.pallas_call(
        _ffn_out_kernel,
        grid=(n // tm,),
        in_specs=in_specs,
        out_specs=out_specs,
        out_shape=out_shape,
        scratch_shapes=[pltpu.VMEM((tm, d), F32)],
        compiler_params=_params(1),
        name="ffn_out",
    )(*args)
    return (outs[0], outs[1]) if gnext is not None else (outs[0], None)


def kernel(x, norm_mix_pre, norm_mix_post, norm_ffn_pre, norm_ffn_post, w_in, b_gate, conv_w,
           conv_b, conv_ln_g, conv_ln_b, w_a_out, sgu_ln_g, sgu_ln_b, w_spatial, b_spatial,
           w_b_out, w_o, w_gate_up, w_down):
    bsz, seq, d = x.shape
    depth = w_in.shape[0]
    n = bsz * seq
    ffn = w_down.shape[1]
    gc = d // SGU_GROUPS

    w_a_b = w_a_out.astype(BF16)
    w_b_b = w_b_out.astype(BF16)
    w_o_b = w_o.astype(BF16)
    w_d_b = w_down.astype(BF16)

    xf = x.reshape(n, d)
    h = _rms_call(xf, norm_mix_pre[0][None])
    for l in range(depth):
        glu = _proj_call(h, w_in, l, (0, d), d, _glu_epilogue, tm=2048, tn=512, name="proj_glu")
        z = _proj_call(h, w_in, l, (2 * d,), 2 * d, _gelu_epilogue, tm=2048, tn=1024,
                       name="proj_gelu")
        gates = _proj_call(h, w_in, l, (4 * d,), 2 * d, _gate_epilogue,
                           bias=b_gate[l].reshape(1, 2 * d), tm=2048, tn=1024, name="proj_gate")
        bsp = jnp.broadcast_to(b_spatial[l][:, :, None], (SGU_GROUPS, CHUNK, gc))
        a_act, b_act = _branch_act_call(
            glu.reshape(bsz, seq, d), conv_w[l], conv_b[l][None], conv_ln_g[l][None],
            conv_ln_b[l][None], z.reshape(bsz, seq, 2 * d), sgu_ln_g[l][None], sgu_ln_b[l][None],
            w_spatial[l], bsp)
        a_act, b_act = a_act.reshape(n, d), b_act.reshape(n, d)
        xf, h = _mix_out_call(a_act, b_act, gates, xf, w_a_b, w_b_b, w_o_b, l,
                              norm_mix_post[l][None], norm_ffn_pre[l][None])
        act = _proj_call(h, w_gate_up, l, (0, ffn), ffn, _swiglu_epilogue, tm=2048, tn=512,
                         name="proj_swiglu")
        gnext = norm_mix_pre[l + 1][None] if l + 1 < depth else None
        xf, h = _ffn_out_call(act, xf, w_d_b, l, norm_ffn_post[l][None], gnext)
    return xf.reshape(bsz, seq, d)
```

```python
import functools
import math

import jax
import jax.numpy as jnp
from jax import lax
from jax.experimental import pallas as pl
from jax.experimental.pallas import tpu as pltpu

F32 = jnp.float32
BF16 = jnp.bfloat16

EPS = 1e-6
CONV_WIDTH = 31
SGU_GROUPS = 8
CHUNK = 128
LANES = 128
SUBLANES = 8
BF16_ROWS = 16
PIECE = 256
HALO = 32
VMEM_LIMIT_BYTES = 56 * 1024 * 1024


def _params(n_axes):
    return pltpu.CompilerParams(
        dimension_semantics=("arbitrary",) * n_axes,
        vmem_limit_bytes=VMEM_LIMIT_BYTES)


def _rms(x, g):
    ms = jnp.mean(x * x, axis=-1, keepdims=True)
    return x * lax.rsqrt(ms + EPS) * g


def _layer_norm(x, g, b):
    mu = jnp.mean(x, axis=-1, keepdims=True)
    xc = x - mu
    var = jnp.mean(xc * xc, axis=-1, keepdims=True)
    return xc * lax.rsqrt(var + EPS) * g + b


def _sigmoid(x):
    return 1.0 / (1.0 + jnp.exp(-x))


def _rms_kernel(x_ref, g_ref, o_ref):
    o_ref[...] = _rms(x_ref[...], g_ref[...]).astype(o_ref.dtype)


def _rms_call(x, g, *, tm=512):
    n, d = x.shape
    return pl.pallas_call(
        _rms_kernel,
        grid=(n // tm,),
        in_specs=[pl.BlockSpec((tm, d), lambda i: (i, 0)),
                  pl.BlockSpec((1, d), lambda i: (0, 0))],
        out_specs=pl.BlockSpec((tm, d), lambda i: (i, 0)),
        out_shape=jax.ShapeDtypeStruct((n, d), BF16),
        compiler_params=_params(1),
        name="rms0",
    )(x, g)


def _glu_epilogue(a, g):
    return a * _sigmoid(g)


def _gelu_epilogue(x):
    return 0.5 * x * (1.0 + lax.erf(x * (1.0 / math.sqrt(2.0))))


def _gate_epilogue(x, bias):
    return _sigmoid(x + bias)


def _swiglu_epilogue(g, u):
    return g * _sigmoid(g) * u


def _proj_kernel(*refs, n_w, has_bias, epilogue):
    h_ref = refs[0]
    w_refs = refs[1:1 + n_w]
    o_ref = refs[1 + n_w + int(has_bias)]
    wb_ref = refs[2 + n_w + int(has_bias)]
    tn = o_ref.shape[1]
    n_pieces = tn // PIECE

    @pl.when(pl.program_id(1) == 0)
    def _():
        for p in range(n_pieces):
            for q, w in enumerate(w_refs):
                dst = (p * n_w + q) * PIECE
                wb_ref[:, dst:dst + PIECE] = w[:, p * PIECE:(p + 1) * PIECE].astype(BF16)

    acc = jnp.dot(h_ref[...], wb_ref[...], preferred_element_type=F32)
    if n_w == 1:
        args = [acc] + ([refs[1 + n_w][...]] if has_bias else [])
        o_ref[...] = epilogue(*args).astype(o_ref.dtype)
        return
    for p in range(n_pieces):
        cols = slice(p * PIECE, (p + 1) * PIECE)
        args = [acc[:, (p * n_w + q) * PIECE:(p * n_w + q + 1) * PIECE] for q in range(n_w)]
        if has_bias:
            args.append(refs[1 + n_w][:, cols])
        o_ref[:, cols] = epilogue(*args).astype(o_ref.dtype)


def _proj_call(h, w_stack, layer, col_offsets, n_out, epilogue, *, bias=None, tm, tn, name,
               single_buffer_weights=False):
    n, k = h.shape
    in_specs = [pl.BlockSpec((tm, k), lambda j, i: (i, 0))]
    mode = dict(pipeline_mode=pl.Buffered(1)) if single_buffer_weights else {}
    for off in col_offsets:
        in_specs.append(pl.BlockSpec(
            (None, k, tn), lambda j, i, off=off: (layer, 0, j + off // tn), **mode))
    args = [h] + [w_stack] * len(col_offsets)
    if bias is not None:
        in_specs.append(pl.BlockSpec((1, tn), lambda j, i: (0, j)))
        args.append(bias)
    kern = functools.partial(_proj_kernel, n_w=len(col_offsets),
                             has_bias=bias is not None, epilogue=epilogue)
    return pl.pallas_call(
        kern,
        grid=(n_out // tn, n // tm),
        in_specs=in_specs,
        out_specs=pl.BlockSpec((tm, tn), lambda j, i: (i, j)),
        out_shape=jax.ShapeDtypeStruct((n, n_out), BF16),
        scratch_shapes=[pltpu.VMEM((k, tn * len(col_offsets)), BF16)],
        compiler_params=_params(2),
        name=name,
    )(*args)


def _conv_tile(xs_ref, w_ref, y_ref, t):
    c = y_ref.shape[-1]
    first = HALO - (CONV_WIDTH - 1)
    last = first + CONV_WIDTH - 1
    n_blocks = t // SUBLANES
    sub = lax.broadcasted_iota(jnp.int32, (SUBLANES, LANES), 0)

    def lane_chunk(ci, carry):
        c0 = pl.multiple_of(ci * LANES, LANES)
        taps = {o: jnp.broadcast_to(w_ref[pl.ds(o - first, 1), pl.ds(c0, LANES)], (SUBLANES, LANES))
                for o in range(first, last + 1)}
        blocks = {}

        def xs_block(j):
            if j not in blocks:
                blocks[j] = xs_ref[pl.ds(j * SUBLANES, SUBLANES), pl.ds(c0, LANES)]
            return blocks[j]

        def phase_sum(b, j):
            acc = None
            for a in range((last - b) // SUBLANES + 1):
                o = SUBLANES * a + b
                if o < first:
                    continue
                term = xs_block(j + a) * taps[o]
                acc = term if acc is None else acc + term
            return acc

        prev = [phase_sum(b, 0) for b in range(1, SUBLANES)]
        for r in range(n_blocks):
            nxt = [phase_sum(b, r + 1) for b in range(1, SUBLANES)]
            out = phase_sum(0, r)
            for b in range(1, SUBLANES):
                merged = jnp.where(sub >= b, prev[b - 1], nxt[b - 1])
                out = out + pltpu.roll(merged, SUBLANES - b, axis=0)
            y_ref[pl.ds(r * SUBLANES, SUBLANES), pl.ds(c0, LANES)] = out
            blocks.pop(r, None)
            prev = nxt
        return carry

    lax.fori_loop(0, c // LANES, lane_chunk, 0)


def _branch_act_kernel(cur_ref, halo_ref, cw_ref, cb_ref, clg_ref, clb_ref, u_ref, v_ref, lg_ref,
                       lb_ref, wsp_ref, bsp_ref, a_ref, b_ref, xs_ref, y_ref, vn_ref, *, t):
    c = cur_ref.shape[-1]
    gc = c // SGU_GROUPS
    s = pl.program_id(1)
    xs_ref[0:HALO, :] = jnp.where(s > 0, halo_ref[...].astype(F32), 0.0)
    xs_ref[HALO:HALO + t, :] = cur_ref[...].astype(F32)
    _conv_tile(xs_ref, cw_ref, y_ref, t)

    cb, clg, clb = cb_ref[...], clg_ref[...], clb_ref[...]
    lg, lb = lg_ref[...], lb_ref[...]
    for r in range(0, t, BF16_ROWS):
        rows = pl.ds(r, BF16_ROWS)
        y = _layer_norm(y_ref[rows, :] + cb, clg, clb)
        a_ref[rows, :] = (y * _sigmoid(y)).astype(a_ref.dtype)
        vn_ref[rows, :] = _layer_norm(v_ref[rows, :].astype(F32), lg, lb).astype(BF16)

    row = lax.broadcasted_iota(jnp.int32, (CHUNK, CHUNK), 0)
    col = lax.broadcasted_iota(jnp.int32, (CHUNK, CHUNK), 1)
    causal = col <= row
    for g in range(SGU_GROUPS):
        w = jnp.where(causal, wsp_ref[g], 0.0).astype(BF16)
        bias = bsp_ref[g]
        for n in range(t // CHUNK):
            rows = slice(n * CHUNK, (n + 1) * CHUNK)
            cols = slice(g * gc, (g + 1) * gc)
            mixed = jnp.dot(w, vn_ref[rows, cols], preferred_element_type=F32) + bias
            b_ref[rows, cols] = (u_ref[rows, cols].astype(F32) * mixed).astype(b_ref.dtype)


def _branch_act_call(glu, conv_w, conv_b, conv_ln_g, conv_ln_b, z, ln_g, ln_b, w_spatial,
                     b_spatial_b, *, t=256):
    bsz, seq, c = glu.shape
    gc = c // SGU_GROUPS
    halo_blocks = t // HALO
    vec = lambda: pl.BlockSpec((1, c), lambda bi, si: (0, 0))
    tile = lambda col: pl.BlockSpec((None, t, c), lambda bi, si: (bi, si, col))
    return pl.pallas_call(
        functools.partial(_branch_act_kernel, t=t),
        grid=(bsz, seq // t),
        in_specs=[
            tile(0),
            pl.BlockSpec((None, HALO, c),
                         lambda bi, si: (bi, jnp.maximum(si * halo_blocks - 1, 0), 0)),
            pl.BlockSpec((CONV_WIDTH, c), lambda bi, si: (0, 0)),
            vec(), vec(), vec(),
            tile(0), tile(1), vec(), vec(),
            pl.BlockSpec((SGU_GROUPS, CHUNK, CHUNK), lambda bi, si: (0, 0, 0)),
            pl.BlockSpec((SGU_GROUPS, CHUNK, gc), lambda bi, si: (0, 0, 0)),
        ],
        out_specs=[tile(0), tile(0)],
        out_shape=[jax.ShapeDtypeStruct((bsz, seq, c), BF16),
                   jax.ShapeDtypeStruct((bsz, seq, c), BF16)],
        scratch_shapes=[pltpu.VMEM((HALO + t, c), F32), pltpu.VMEM((t, c), F32),
                        pltpu.VMEM((t, c), BF16)],
        compiler_params=_params(2),
        name="branch_act",
    )(glu, glu, conv_w, conv_b, conv_ln_g, conv_ln_b, z, z, ln_g, ln_b, w_spatial, b_spatial_b)


def _residual_out(x_ref, o_ref, gpost_ref, gnext_ref, xo_ref, ho_ref):
    gpost = gpost_ref[...]
    gnext = None if ho_ref is None else gnext_ref[...]
    for r in range(0, x_ref.shape[0], BF16_ROWS):
        rows = pl.ds(r, BF16_ROWS)
        xn = x_ref[rows, :] + _rms(o_ref[rows, :], gpost)
        xo_ref[rows, :] = xn
        if ho_ref is not None:
            ho_ref[rows, :] = _rms(xn, gnext).astype(ho_ref.dtype)


def _mix_out_kernel(a_ref, b_ref, g0_ref, g1_ref, x_ref, wa_ref, wb_ref, wo_ref,
                    gpost_ref, gnext_ref, xo_ref, ho_ref, o_scr):
    ya = jnp.dot(a_ref[...], wa_ref[...], preferred_element_type=F32)
    yb = jnp.dot(b_ref[...], wb_ref[...], preferred_element_type=F32)
    merged = g0_ref[...].astype(F32) * ya + g1_ref[...].astype(F32) * yb
    o_scr[...] = jnp.dot(merged.astype(BF16), wo_ref[...], preferred_element_type=F32)
    _residual_out(x_ref, o_scr, gpost_ref, gnext_ref, xo_ref, ho_ref)


def _ffn_out_kernel(act_ref, x_ref, wd_ref, gpost_ref, *rest):
    gnext_ref, xo_ref, ho_ref, o_scr = rest if len(rest) == 4 else (None, rest[0], None, rest[1])
    o_scr[...] = jnp.dot(act_ref[...], wd_ref[...], preferred_element_type=F32)
    _residual_out(x_ref, o_scr, gpost_ref, gnext_ref, xo_ref, ho_ref)


def _resident(shape, index):
    return pl.BlockSpec(shape, index, pipeline_mode=pl.Buffered(1))


def _mix_out_call(a_act, b_act, gates, x, wa, wb, wo, layer, gpost, gnext, *, tm=256):
    n, d = x.shape
    row = lambda: pl.BlockSpec((tm, d), lambda i: (i, 0))
    vec = lambda: pl.BlockSpec((1, d), lambda i: (0, 0))
    wspec = lambda w: _resident((None,) + w.shape[1:], lambda i: (layer, 0, 0))
    return pl.pallas_call(
        _mix_out_kernel,
        grid=(n // tm,),
        in_specs=[row(), row(), row(), pl.BlockSpec((tm, d), lambda i: (i, 1)), row(),
                  wspec(wa), wspec(wb), wspec(wo), vec(), vec()],
        out_specs=[row(), row()],
        out_shape=[jax.ShapeDtypeStruct((n, d), F32), jax.ShapeDtypeStruct((n, d), BF16)],
        scratch_shapes=[pltpu.VMEM((tm, d), F32)],
        compiler_params=_params(1),
        name="mix_out",
    )(a_act, b_act, gates, gates, x, wa, wb, wo, gpost, gnext)


def _ffn_out_call(act, x, wd, layer, gpost, gnext, *, tm=256):
    n, d = x.shape
    f = act.shape[1]
    row = lambda: pl.BlockSpec((tm, d), lambda i: (i, 0))
    vec = lambda: pl.BlockSpec((1, d), lambda i: (0, 0))
    in_specs = [pl.BlockSpec((tm, f), lambda i: (i, 0)), row(),
                _resident((None, f, d), lambda i: (layer, 0, 0)), vec()]
    args = [act, x, wd, gpost]
    out_specs = [row()]
    out_shape = [jax.ShapeDtypeStruct((n, d), F32)]
    if gnext is not None:
        in_specs.append(vec())
        args.append(gnext)
        out_specs.append(row())
        out_shape.append(jax.ShapeDtypeStruct((n, d), BF16))
    outs = pl.pallas_call(
        _ffn_out_kernel,
        grid=(n // tm,),
        in_specs=in_specs,
        out_specs=out_specs,
        out_shape=out_shape,
        scratch_shapes=[pltpu.VMEM((tm, d), F32)],
        compiler_params=_params(1),
        name="ffn_out",
    )(*args)
    return (outs[0], outs[1]) if gnext is not None else (outs[0], None)


def kernel(x, norm_mix_pre, norm_mix_post, norm_ffn_pre, norm_ffn_post, w_in, b_gate, conv_w,
           conv_b, conv_ln_g, conv_ln_b, w_a_out, sgu_ln_g, sgu_ln_b, w_spatial, b_spatial,
           w_b_out, w_o, w_gate_up, w_down):
    bsz, seq, d = x.shape
    depth = w_in.shape[0]
    n = bsz * seq
    ffn = w_down.shape[1]
    gc = d // SGU_GROUPS

    w_a_b = w_a_out.astype(BF16)
    w_b_b = w_b_out.astype(BF16)
    w_o_b = w_o.astype(BF16)
    w_d_b = w_down.astype(BF16)

    xf = x.reshape(n, d)
    h = _rms_call(xf, norm_mix_pre[0][None])
    for l in range(depth):
        glu = _proj_call(h, w_in, l, (0, d), d, _glu_epilogue, tm=2048, tn=512, name="proj_glu")
        z = _proj_call(h, w_in, l, (2 * d,), 2 * d, _gelu_epilogue, tm=1024, tn=2048,
                       name="proj_gelu", single_buffer_weights=True)
        gates = _proj_call(h, w_in, l, (4 * d,), 2 * d, _gate_epilogue,
                           bias=b_gate[l].reshape(1, 2 * d), tm=1024, tn=2048, name="proj_gate",
                           single_buffer_weights=True)
        bsp = jnp.broadcast_to(b_spatial[l][:, :, None], (SGU_GROUPS, CHUNK, gc))
        a_act, b_act = _branch_act_call(
            glu.reshape(bsz, seq, d), conv_w[l], conv_b[l][None], conv_ln_g[l][None],
            conv_ln_b[l][None], z.reshape(bsz, seq, 2 * d), sgu_ln_g[l][None], sgu_ln_b[l][None],
            w_spatial[l], bsp)
        a_act, b_act = a_act.reshape(n, d), b_act.reshape(n, d)
        xf, h = _mix_out_call(a_act, b_act, gates, xf, w_a_b, w_b_b, w_o_b, l,
                              norm_mix_post[l][None], norm_ffn_pre[l][None])
        act = _proj_call(h, w_gate_up, l, (0, ffn), ffn, _swiglu_epilogue, tm=2048, tn=512,
                         name="proj_swiglu")
        gnext = norm_mix_pre[l + 1][None] if l + 1 < depth else None
        xf, h = _ffn_out_call(act, xf, w_d_b, l, norm_ffn_post[l][None], gnext)
    return xf.reshape(bsz, seq, d)
```

```python
import functools
import math

import jax
import jax.numpy as jnp
from jax import lax
from jax.experimental import pallas as pl
from jax.experimental.pallas import tpu as pltpu

F32 = jnp.float32
BF16 = jnp.bfloat16

EPS = 1e-6
CONV_WIDTH = 31
SGU_GROUPS = 8
CHUNK = 128
LANES = 128
SUBLANES = 8
BF16_ROWS = 16
PIECE = 256
HALO = 32
VMEM_LIMIT_BYTES = 56 * 1024 * 1024


def _params(n_axes):
    return pltpu.CompilerParams(
        dimension_semantics=("arbitrary",) * n_axes,
        vmem_limit_bytes=VMEM_LIMIT_BYTES)


def _rms(x, g):
    ms = jnp.mean(x * x, axis=-1, keepdims=True)
    return x * lax.rsqrt(ms + EPS) * g


def _layer_norm(x, g, b):
    mu = jnp.mean(x, axis=-1, keepdims=True)
    xc = x - mu
    var = jnp.mean(xc * xc, axis=-1, keepdims=True)
    return xc * lax.rsqrt(var + EPS) * g + b


def _sigmoid(x):
    return 1.0 / (1.0 + jnp.exp(-x))


def _rms_kernel(x_ref, g_ref, o_ref):
    o_ref[...] = _rms(x_ref[...], g_ref[...]).astype(o_ref.dtype)


def _rms_call(x, g, *, tm=512):
    n, d = x.shape
    return pl.pallas_call(
        _rms_kernel,
        grid=(n // tm,),
        in_specs=[pl.BlockSpec((tm, d), lambda i: (i, 0)),
                  pl.BlockSpec((1, d), lambda i: (0, 0))],
        out_specs=pl.BlockSpec((tm, d), lambda i: (i, 0)),
        out_shape=jax.ShapeDtypeStruct((n, d), BF16),
        compiler_params=_params(1),
        name="rms0",
    )(x, g)


def _glu_epilogue(a, g):
    return a * _sigmoid(g)


def _gelu_epilogue(x):
    return 0.5 * x * (1.0 + lax.erf(x * (1.0 / math.sqrt(2.0))))


def _gate_epilogue(x, bias):
    return _sigmoid(x + bias)


def _swiglu_epilogue(g, u):
    return g * _sigmoid(g) * u


def _round_interleaved(w_refs, wb_ref):
    n_w = len(w_refs)
    for p in range(w_refs[0].shape[1] // PIECE):
        for q, w in enumerate(w_refs):
            dst = (p * n_w + q) * PIECE
            wb_ref[:, dst:dst + PIECE] = w[:, p * PIECE:(p + 1) * PIECE].astype(BF16)


def _proj_kernel(*refs, n_w, has_bias, epilogue):
    h_ref = refs[0]
    w_refs = refs[1:1 + n_w]
    o_ref = refs[1 + n_w + int(has_bias)]
    wb_ref = refs[2 + n_w + int(has_bias)]
    tn = o_ref.shape[1]

    @pl.when(pl.program_id(1) == 0)
    def _():
        _round_interleaved(w_refs, wb_ref)

    acc = jnp.dot(h_ref[...], wb_ref[...], preferred_element_type=F32)
    if n_w == 1:
        args = [acc] + ([refs[1 + n_w][...]] if has_bias else [])
        o_ref[...] = epilogue(*args).astype(o_ref.dtype)
        return
    for p in range(tn // PIECE):
        cols = slice(p * PIECE, (p + 1) * PIECE)
        args = [acc[:, (p * n_w + q) * PIECE:(p * n_w + q + 1) * PIECE] for q in range(n_w)]
        if has_bias:
            args.append(refs[1 + n_w][:, cols])
        o_ref[:, cols] = epilogue(*args).astype(o_ref.dtype)


def _proj_call(h, w_stack, layer, col_offsets, n_out, epilogue, *, bias=None, tm, tn, name):
    n, k = h.shape
    in_specs = [pl.BlockSpec((tm, k), lambda j, i: (i, 0))]
    for off in col_offsets:
        in_specs.append(pl.BlockSpec(
            (None, k, tn), lambda j, i, off=off: (layer, 0, j + off // tn)))
    args = [h] + [w_stack] * len(col_offsets)
    if bias is not None:
        in_specs.append(pl.BlockSpec((1, tn), lambda j, i: (0, j)))
        args.append(bias)
    kern = functools.partial(_proj_kernel, n_w=len(col_offsets),
                             has_bias=bias is not None, epilogue=epilogue)
    return pl.pallas_call(
        kern,
        grid=(n_out // tn, n // tm),
        in_specs=in_specs,
        out_specs=pl.BlockSpec((tm, tn), lambda j, i: (i, j)),
        out_shape=jax.ShapeDtypeStruct((n, n_out), BF16),
        scratch_shapes=[pltpu.VMEM((k, tn * len(col_offsets)), BF16)],
        compiler_params=_params(2),
        name=name,
    )(*args)


def _conv_columns(halo, cur, taps_ref, y_ref, col0):
    first = HALO - (CONV_WIDTH - 1)
    last = first + CONV_WIDTH - 1
    halo_blocks = HALO // SUBLANES
    n_blocks = cur.shape[0] // SUBLANES
    sub = lax.broadcasted_iota(jnp.int32, (SUBLANES, LANES), 0)
    lanes = pl.ds(col0, LANES)
    taps = {o: jnp.broadcast_to(taps_ref[pl.ds(o - first, 1), lanes], (SUBLANES, LANES))
            for o in range(first, last + 1)}

    def x_block(j):
        if j < halo_blocks:
            return halo[j * SUBLANES:(j + 1) * SUBLANES, :]
        j -= halo_blocks
        return cur[j * SUBLANES:(j + 1) * SUBLANES, :]

    def phase_sum(b, j):
        acc = None
        for a in range((last - b) // SUBLANES + 1):
            o = SUBLANES * a + b
            if o < first:
                continue
            term = x_block(j + a) * taps[o]
            acc = term if acc is None else acc + term
        return acc

    prev = [phase_sum(b, 0) for b in range(1, SUBLANES)]
    for r in range(n_blocks):
        nxt = [phase_sum(b, r + 1) for b in range(1, SUBLANES)]
        out = phase_sum(0, r)
        for b in range(1, SUBLANES):
            merged = jnp.where(sub >= b, prev[b - 1], nxt[b - 1])
            out = out + pltpu.roll(merged, SUBLANES - b, axis=0)
        y_ref[pl.ds(r * SUBLANES, SUBLANES), lanes] = out
        prev = nxt


def _glu_conv_kernel(h_ref, wa_ref, wg_ref, cw_ref, y_ref, wb_ref, carry_ref, *, tiles_per_seq):
    i = pl.program_id(1)
    tm, tn = y_ref.shape

    @pl.when(i == 0)
    def _():
        carry_ref[...] = jnp.zeros(carry_ref.shape, F32)
        _round_interleaved((wa_ref, wg_ref), wb_ref)

    acc = jnp.dot(h_ref[...], wb_ref[...], preferred_element_type=F32)
    seq_start = lax.rem(i, tiles_per_seq) == 0
    for p in range(tn // PIECE):
        glu = _glu_epilogue(acc[:, 2 * p * PIECE:(2 * p + 1) * PIECE],
                            acc[:, (2 * p + 1) * PIECE:(2 * p + 2) * PIECE])
        cols = slice(p * PIECE, (p + 1) * PIECE)
        halo = jnp.where(seq_start, 0.0, carry_ref[:, cols])
        for lc in range(PIECE // LANES):
            ls = slice(lc * LANES, (lc + 1) * LANES)
            _conv_columns(halo[:, ls], glu[:, ls], cw_ref, y_ref, p * PIECE + lc * LANES)
        carry_ref[:, cols] = glu[tm - HALO:, :]


def _glu_conv_call(h, w_stack, layer, d, conv_w, seq, *, tm=512, tn=512):
    n, k = h.shape
    wspec = lambda off: pl.BlockSpec((None, k, tn), lambda j, i: (layer, 0, j + off // tn))
    return pl.pallas_call(
        functools.partial(_glu_conv_kernel, tiles_per_seq=seq // tm),
        grid=(d // tn, n // tm),
        in_specs=[pl.BlockSpec((tm, k), lambda j, i: (i, 0)), wspec(0), wspec(d),
                  pl.BlockSpec((CONV_WIDTH, tn), lambda j, i: (0, j))],
        out_specs=pl.BlockSpec((tm, tn), lambda j, i: (i, j)),
        out_shape=jax.ShapeDtypeStruct((n, d), F32),
        scratch_shapes=[pltpu.VMEM((k, 2 * tn), BF16), pltpu.VMEM((HALO, tn), F32)],
        compiler_params=_params(2),
        name="proj_glu_conv",
    )(h, w_stack, w_stack, conv_w)


def _branch_act_kernel(y_ref, cb_ref, clg_ref, clb_ref, u_ref, v_ref, lg_ref, lb_ref, wsp_ref,
                       bsp_ref, a_ref, b_ref, vn_ref, *, t):
    c = u_ref.shape[-1]
    gc = c // SGU_GROUPS
    cb, clg, clb = cb_ref[...], clg_ref[...], clb_ref[...]
    lg, lb = lg_ref[...], lb_ref[...]
    for r in range(0, t, BF16_ROWS):
        rows = pl.ds(r, BF16_ROWS)
        y = _layer_norm(y_ref[rows, :] + cb, clg, clb)
        a_ref[rows, :] = (y * _sigmoid(y)).astype(a_ref.dtype)
        vn_ref[rows, :] = _layer_norm(v_ref[rows, :].astype(F32), lg, lb).astype(BF16)

    row = lax.broadcasted_iota(jnp.int32, (CHUNK, CHUNK), 0)
    col = lax.broadcasted_iota(jnp.int32, (CHUNK, CHUNK), 1)
    causal = col <= row
    for g in range(SGU_GROUPS):
        w = jnp.where(causal, wsp_ref[g], 0.0).astype(BF16)
        bias = bsp_ref[g]
        for n in range(t // CHUNK):
            rows = slice(n * CHUNK, (n + 1) * CHUNK)
            cols = slice(g * gc, (g + 1) * gc)
            mixed = jnp.dot(w, vn_ref[rows, cols], preferred_element_type=F32) + bias
            b_ref[rows, cols] = (u_ref[rows, cols].astype(F32) * mixed).astype(b_ref.dtype)


def _branch_act_call(y, conv_b, conv_ln_g, conv_ln_b, z, ln_g, ln_b, w_spatial, b_spatial_b, *, t=256):
    n, c = y.shape
    gc = c // SGU_GROUPS
    vec = lambda: pl.BlockSpec((1, c), lambda i: (0, 0))
    tile = lambda col: pl.BlockSpec((t, c), lambda i: (i, col))
    return pl.pallas_call(
        functools.partial(_branch_act_kernel, t=t),
        grid=(n // t,),
        in_specs=[tile(0), vec(), vec(), vec(), tile(0), tile(1), vec(), vec(),
                  pl.BlockSpec((SGU_GROUPS, CHUNK, CHUNK), lambda i: (0, 0, 0)),
                  pl.BlockSpec((SGU_GROUPS, CHUNK, gc), lambda i: (0, 0, 0))],
        out_specs=[tile(0), tile(0)],
        out_shape=[jax.ShapeDtypeStruct((n, c), BF16), jax.ShapeDtypeStruct((n, c), BF16)],
        scratch_shapes=[pltpu.VMEM((t, c), BF16)],
        compiler_params=_params(1),
        name="branch_act",
    )(y, conv_b, conv_ln_g, conv_ln_b, z, z, ln_g, ln_b, w_spatial, b_spatial_b)


def _residual_out(x_ref, o_ref, gpost_ref, gnext_ref, xo_ref, ho_ref):
    gpost = gpost_ref[...]
    gnext = None if ho_ref is None else gnext_ref[...]
    for r in range(0, x_ref.shape[0], BF16_ROWS):
        rows = pl.ds(r, BF16_ROWS)
        xn = x_ref[rows, :] + _rms(o_ref[rows, :], gpost)
        xo_ref[rows, :] = xn
        if ho_ref is not None:
            ho_ref[rows, :] = _rms(xn, gnext).astype(ho_ref.dtype)


def _mix_out_kernel(a_ref, b_ref, g0_ref, g1_ref, x_ref, wa_ref, wb_ref, wo_ref,
                    gpost_ref, gnext_ref, xo_ref, ho_ref, o_scr):
    ya = jnp.dot(a_ref[...], wa_ref[...], preferred_element_type=F32)
    yb = jnp.dot(b_ref[...], wb_ref[...], preferred_element_type=F32)
    merged = g0_ref[...].astype(F32) * ya + g1_ref[...].astype(F32) * yb
    o_scr[...] = jnp.dot(merged.astype(BF16), wo_ref[...], preferred_element_type=F32)
    _residual_out(x_ref, o_scr, gpost_ref, gnext_ref, xo_ref, ho_ref)


def _ffn_out_kernel(act_ref, x_ref, wd_ref, gpost_ref, *rest):
    gnext_ref, xo_ref, ho_ref, o_scr = rest if len(rest) == 4 else (None, rest[0], None, rest[1])
    o_scr[...] = jnp.dot(act_ref[...], wd_ref[...], preferred_element_type=F32)
    _residual_out(x_ref, o_scr, gpost_ref, gnext_ref, xo_ref, ho_ref)


def _resident(shape, index):
    return pl.BlockSpec(shape, index, pipeline_mode=pl.Buffered(1))


def _mix_out_call(a_act, b_act, gates, x, wa, wb, wo, layer, gpost, gnext, *, tm=256):
    n, d = x.shape
    row = lambda: pl.BlockSpec((tm, d), lambda i: (i, 0))
    vec = lambda: pl.BlockSpec((1, d), lambda i: (0, 0))
    wspec = lambda w: _resident((None,) + w.shape[1:], lambda i: (layer, 0, 0))
    return pl.pallas_call(
        _mix_out_kernel,
        grid=(n // tm,),
        in_specs=[row(), row(), row(), pl.BlockSpec((tm, d), lambda i: (i, 1)), row(),
                  wspec(wa), wspec(wb), wspec(wo), vec(), vec()],
        out_specs=[row(), row()],
        out_shape=[jax.ShapeDtypeStruct((n, d), F32), jax.ShapeDtypeStruct((n, d), BF16)],
        scratch_shapes=[pltpu.VMEM((tm, d), F32)],
        compiler_params=_params(1),
        name="mix_out",
    )(a_act, b_act, gates, gates, x, wa, wb, wo, gpost, gnext)


def _ffn_out_call(act, x, wd, layer, gpost, gnext, *, tm=256):
    n, d = x.shape
    f = act.shape[1]
    row = lambda: pl.BlockSpec((tm, d), lambda i: (i, 0))
    vec = lambda: pl.BlockSpec((1, d), lambda i: (0, 0))
    in_specs = [pl.BlockSpec((tm, f), lambda i: (i, 0)), row(),
                _resident((None, f, d), lambda i: (layer, 0, 0)), vec()]
    args = [act, x, wd, gpost]
    out_specs = [row()]
    out_shape = [jax.ShapeDtypeStruct((n, d), F32)]
    if gnext is not None:
        in_specs.append(vec())
        args.append(gnext)
        out_specs.append(row())
        out_shape.append(jax.ShapeDtypeStruct((n, d), BF16))
    outs = pl.pallas_call(
        _ffn_out_kernel,
        grid=(n // tm,),
        in_specs=in_specs,
        out_specs=out_specs,
        out_shape=out_shape,
        scratch_shapes=[pltpu.VMEM((tm, d), F32)],
        compiler_params=_params(1),
        name="ffn_out",
    )(*args)
    return (outs[0], outs[1]) if gnext is not None else (outs[0], None)


def kernel(x, norm_mix_pre, norm_mix_post, norm_ffn_pre, norm_ffn_post, w_in, b_gate, conv_w,
           conv_b, conv_ln_g, conv_ln_b, w_a_out, sgu_ln_g, sgu_ln_b, w_spatial, b_spatial,
           w_b_out, w_o, w_gate_up, w_down):
    bsz, seq, d = x.shape
    depth = w_in.shape[0]
    n = bsz * seq
    ffn = w_down.shape[1]
    gc = d // SGU_GROUPS

    w_a_b = w_a_out.astype(BF16)
    w_b_b = w_b_out.astype(BF16)
    w_o_b = w_o.astype(BF16)
    w_d_b = w_down.astype(BF16)

    xf = x.reshape(n, d)
    h = _rms_call(xf, norm_mix_pre[0][None])
    for l in range(depth):
        y = _glu_conv_call(h, w_in, l, d, conv_w[l], seq)
        z = _proj_call(h, w_in, l, (2 * d,), 2 * d, _gelu_epilogue, tm=2048, tn=1024,
                       name="proj_gelu")
        gates = _proj_call(h, w_in, l, (4 * d,), 2 * d, _gate_epilogue,
                           bias=b_gate[l].reshape(1, 2 * d), tm=2048, tn=1024, name="proj_gate")
        bsp = jnp.broadcast_to(b_spatial[l][:, :, None], (SGU_GROUPS, CHUNK, gc))
        a_act, b_act = _branch_act_call(y, conv_b[l][None], conv_ln_g[l][None], conv_ln_b[l][None], z,
                                        sgu_ln_g[l][None], sgu_ln_b[l][None], w_spatial[l], bsp)
        xf, h = _mix_out_call(a_act, b_act, gates, xf, w_a_b, w_b_b, w_o_b, l,
                              norm_mix_post[l][None], norm_ffn_pre[l][None])
        act = _proj_call(h, w_gate_up, l, (0, ffn), ffn, _swiglu_epilogue, tm=2048, tn=512,
                         name="proj_swiglu")
        gnext = norm_mix_pre[l + 1][None] if l + 1 < depth else None
        xf, h = _ffn_out_call(act, xf, w_d_b, l, norm_ffn_post[l][None], gnext)
    return xf.reshape(bsz, seq, d)
```

```python
import functools
import math

import jax
import jax.numpy as jnp
from jax import lax
from jax.experimental import pallas as pl
from jax.experimental.pallas import tpu as pltpu

F32 = jnp.float32
BF16 = jnp.bfloat16

EPS = 1e-6
CONV_WIDTH = 31
SGU_GROUPS = 8
CHUNK = 128
LANES = 128
SUBLANES = 8
BF16_ROWS = 16
PIECE = 256
HALO = 32
VMEM_LIMIT_BYTES = 56 * 1024 * 1024


def _params(n_axes):
    return pltpu.CompilerParams(
        dimension_semantics=("arbitrary",) * n_axes,
        vmem_limit_bytes=VMEM_LIMIT_BYTES)


def _rms(x, g):
    ms = jnp.mean(x * x, axis=-1, keepdims=True)
    return x * lax.rsqrt(ms + EPS) * g


def _layer_norm(x, g, b):
    mu = jnp.mean(x, axis=-1, keepdims=True)
    xc = x - mu
    var = jnp.mean(xc * xc, axis=-1, keepdims=True)
    return xc * lax.rsqrt(var + EPS) * g + b


def _sigmoid(x):
    return 1.0 / (1.0 + jnp.exp(-x))


def _rms_kernel(x_ref, g_ref, o_ref):
    o_ref[...] = _rms(x_ref[...], g_ref[...]).astype(o_ref.dtype)


def _rms_call(x, g, *, tm=512):
    n, d = x.shape
    return pl.pallas_call(
        _rms_kernel,
        grid=(n // tm,),
        in_specs=[pl.BlockSpec((tm, d), lambda i: (i, 0)),
                  pl.BlockSpec((1, d), lambda i: (0, 0))],
        out_specs=pl.BlockSpec((tm, d), lambda i: (i, 0)),
        out_shape=jax.ShapeDtypeStruct((n, d), BF16),
        compiler_params=_params(1),
        name="rms0",
    )(x, g)


def _glu_epilogue(a, g):
    return a * _sigmoid(g)


def _gelu_epilogue(x):
    return 0.5 * x * (1.0 + lax.erf(x * (1.0 / math.sqrt(2.0))))


def _gate_epilogue(x, bias):
    return _sigmoid(x + bias)


def _swiglu_epilogue(g, u):
    return g * _sigmoid(g) * u


def _proj_kernel(*refs, n_w, has_bias, epilogue):
    h_ref = refs[0]
    w_refs = refs[1:1 + n_w]
    n_in = 2 + n_w + int(has_bias)
    cast_src_ref, o_ref, cast_dst_ref, wb_ref = refs[n_in - 1:n_in + 3]
    tn = o_ref.shape[1]
    n_pieces = tn // PIECE

    cast_dst_ref[...] = cast_src_ref[...].astype(BF16)

    @pl.when(pl.program_id(1) == 0)
    def _():
        for p in range(n_pieces):
            for q, w in enumerate(w_refs):
                dst = (p * n_w + q) * PIECE
                wb_ref[:, dst:dst + PIECE] = w[:, p * PIECE:(p + 1) * PIECE].astype(BF16)

    acc = jnp.dot(h_ref[...], wb_ref[...], preferred_element_type=F32)
    if n_w == 1:
        args = [acc] + ([refs[1 + n_w][...]] if has_bias else [])
        o_ref[...] = epilogue(*args).astype(o_ref.dtype)
        return
    for p in range(n_pieces):
        cols = slice(p * PIECE, (p + 1) * PIECE)
        args = [acc[:, (p * n_w + q) * PIECE:(p * n_w + q + 1) * PIECE] for q in range(n_w)]
        if has_bias:
            args.append(refs[1 + n_w][:, cols])
        o_ref[:, cols] = epilogue(*args).astype(o_ref.dtype)


def _proj_call(h, w_stack, layer, col_offsets, n_out, epilogue, cast_stack, *, bias=None, tm, tn, name):
    n, k = h.shape
    n_col_tiles, n_row_tiles = n_out // tn, n // tm
    rows, cols = cast_stack.shape[1:]
    slab = rows // (n_col_tiles * n_row_tiles)
    assert slab * n_col_tiles * n_row_tiles == rows and slab % BF16_ROWS == 0
    in_specs = [pl.BlockSpec((tm, k), lambda j, i: (i, 0))]
    for off in col_offsets:
        in_specs.append(pl.BlockSpec(
            (None, k, tn), lambda j, i, off=off: (layer, 0, j + off // tn)))
    args = [h] + [w_stack] * len(col_offsets)
    if bias is not None:
        in_specs.append(pl.BlockSpec((1, tn), lambda j, i: (0, j)))
        args.append(bias)
    in_specs.append(pl.BlockSpec((None, slab, cols), lambda j, i: (layer, j * n_row_tiles + i, 0)))
    args.append(cast_stack)
    kern = functools.partial(_proj_kernel, n_w=len(col_offsets), has_bias=bias is not None,
                             epilogue=epilogue)
    return pl.pallas_call(
        kern,
        grid=(n_col_tiles, n_row_tiles),
        in_specs=in_specs,
        out_specs=[pl.BlockSpec((tm, tn), lambda j, i: (i, j)),
                   pl.BlockSpec((slab, cols), lambda j, i: (j * n_row_tiles + i, 0))],
        out_shape=[jax.ShapeDtypeStruct((n, n_out), BF16),
                   jax.ShapeDtypeStruct((rows, cols), BF16)],
        scratch_shapes=[pltpu.VMEM((k, tn * len(col_offsets)), BF16)],
        compiler_params=_params(2),
        name=name,
    )(*args)


def _conv_tile(xs_ref, w_ref, y_ref, t):
    c = y_ref.shape[-1]
    first = HALO - (CONV_WIDTH - 1)
    last = first + CONV_WIDTH - 1
    n_blocks = t // SUBLANES
    sub = lax.broadcasted_iota(jnp.int32, (SUBLANES, LANES), 0)

    def lane_chunk(ci, carry):
        c0 = pl.multiple_of(ci * LANES, LANES)
        taps = {o: jnp.broadcast_to(w_ref[pl.ds(o - first, 1), pl.ds(c0, LANES)], (SUBLANES, LANES))
                for o in range(first, last + 1)}
        blocks = {}

        def xs_block(j):
            if j not in blocks:
                blocks[j] = xs_ref[pl.ds(j * SUBLANES, SUBLANES), pl.ds(c0, LANES)]
            return blocks[j]

        def phase_sum(b, j):
            acc = None
            for a in range((last - b) // SUBLANES + 1):
                o = SUBLANES * a + b
                if o < first:
                    continue
                term = xs_block(j + a) * taps[o]
                acc = term if acc is None else acc + term
            return acc

        prev = [phase_sum(b, 0) for b in range(1, SUBLANES)]
        for r in range(n_blocks):
            nxt = [phase_sum(b, r + 1) for b in range(1, SUBLANES)]
            out = phase_sum(0, r)
            for b in range(1, SUBLANES):
                merged = jnp.where(sub >= b, prev[b - 1], nxt[b - 1])
                out = out + pltpu.roll(merged, SUBLANES - b, axis=0)
            y_ref[pl.ds(r * SUBLANES, SUBLANES), pl.ds(c0, LANES)] = out
            blocks.pop(r, None)
            prev = nxt
        return carry

    lax.fori_loop(0, c // LANES, lane_chunk, 0)


def _branch_act_kernel(cur_ref, halo_ref, cw_ref, cb_ref, clg_ref, clb_ref, u_ref, v_ref, lg_ref,
                       lb_ref, wsp_ref, bsp_ref, a_ref, b_ref, xs_ref, y_ref, vn_ref, *, t):
    c = cur_ref.shape[-1]
    gc = c // SGU_GROUPS
    s = pl.program_id(1)
    xs_ref[0:HALO, :] = jnp.where(s > 0, halo_ref[...].astype(F32), 0.0)
    xs_ref[HALO:HALO + t, :] = cur_ref[...].astype(F32)
    _conv_tile(xs_ref, cw_ref, y_ref, t)

    cb, clg, clb = cb_ref[...], clg_ref[...], clb_ref[...]
    lg, lb = lg_ref[...], lb_ref[...]
    for r in range(0, t, BF16_ROWS):
        rows = pl.ds(r, BF16_ROWS)
        y = _layer_norm(y_ref[rows, :] + cb, clg, clb)
        a_ref[rows, :] = (y * _sigmoid(y)).astype(a_ref.dtype)
        vn_ref[rows, :] = _layer_norm(v_ref[rows, :].astype(F32), lg, lb).astype(BF16)

    row = lax.broadcasted_iota(jnp.int32, (CHUNK, CHUNK), 0)
    col = lax.broadcasted_iota(jnp.int32, (CHUNK, CHUNK), 1)
    causal = col <= row
    for g in range(SGU_GROUPS):
        w = jnp.where(causal, wsp_ref[g], 0.0).astype(BF16)
        bias = bsp_ref[g]
        for n in range(t // CHUNK):
            rows = slice(n * CHUNK, (n + 1) * CHUNK)
            cols = slice(g * gc, (g + 1) * gc)
            mixed = jnp.dot(w, vn_ref[rows, cols], preferred_element_type=F32) + bias
            b_ref[rows, cols] = (u_ref[rows, cols].astype(F32) * mixed).astype(b_ref.dtype)


def _branch_act_call(glu, conv_w, conv_b, conv_ln_g, conv_ln_b, z, ln_g, ln_b, w_spatial,
                     b_spatial_b, *, t=256):
    bsz, seq, c = glu.shape
    gc = c // SGU_GROUPS
    halo_blocks = t // HALO
    vec = lambda: pl.BlockSpec((1, c), lambda bi, si: (0, 0))
    tile = lambda col: pl.BlockSpec((None, t, c), lambda bi, si: (bi, si, col))
    return pl.pallas_call(
        functools.partial(_branch_act_kernel, t=t),
        grid=(bsz, seq // t),
        in_specs=[
            tile(0),
            pl.BlockSpec((None, HALO, c),
                         lambda bi, si: (bi, jnp.maximum(si * halo_blocks - 1, 0), 0)),
            pl.BlockSpec((CONV_WIDTH, c), lambda bi, si: (0, 0)),
            vec(), vec(), vec(),
            tile(0), tile(1), vec(), vec(),
            pl.BlockSpec((SGU_GROUPS, CHUNK, CHUNK), lambda bi, si: (0, 0, 0)),
            pl.BlockSpec((SGU_GROUPS, CHUNK, gc), lambda bi, si: (0, 0, 0)),
        ],
        out_specs=[tile(0), tile(0)],
        out_shape=[jax.ShapeDtypeStruct((bsz, seq, c), BF16),
                   jax.ShapeDtypeStruct((bsz, seq, c), BF16)],
        scratch_shapes=[pltpu.VMEM((HALO + t, c), F32), pltpu.VMEM((t, c), F32),
                        pltpu.VMEM((t, c), BF16)],
        compiler_params=_params(2),
        name="branch_act",
    )(glu, glu, conv_w, conv_b, conv_ln_g, conv_ln_b, z, z, ln_g, ln_b, w_spatial, b_spatial_b)


def _residual_out(x_ref, o_ref, gpost_ref, gnext_ref, xo_ref, ho_ref):
    gpost = gpost_ref[...]
    gnext = None if ho_ref is None else gnext_ref[...]
    for r in range(0, x_ref.shape[0], BF16_ROWS):
        rows = pl.ds(r, BF16_ROWS)
        xn = x_ref[rows, :] + _rms(o_ref[rows, :], gpost)
        xo_ref[rows, :] = xn
        if ho_ref is not None:
            ho_ref[rows, :] = _rms(xn, gnext).astype(ho_ref.dtype)


def _mix_out_kernel(a_ref, b_ref, g0_ref, g1_ref, x_ref, wa_ref, wb_ref, wo_ref,
                    gpost_ref, gnext_ref, xo_ref, ho_ref, o_scr):
    ya = jnp.dot(a_ref[...], wa_ref[...], preferred_element_type=F32)
    yb = jnp.dot(b_ref[...], wb_ref[...], preferred_element_type=F32)
    merged = g0_ref[...].astype(F32) * ya + g1_ref[...].astype(F32) * yb
    o_scr[...] = jnp.dot(merged.astype(BF16), wo_ref[...], preferred_element_type=F32)
    _residual_out(x_ref, o_scr, gpost_ref, gnext_ref, xo_ref, ho_ref)


def _ffn_out_kernel(act_ref, x_ref, wd_ref, gpost_ref, *rest):
    gnext_ref, xo_ref, ho_ref, o_scr = rest if len(rest) == 4 else (None, rest[0], None, rest[1])
    o_scr[...] = jnp.dot(act_ref[...], wd_ref[...], preferred_element_type=F32)
    _residual_out(x_ref, o_scr, gpost_ref, gnext_ref, xo_ref, ho_ref)


def _resident(shape, index):
    return pl.BlockSpec(shape, index, pipeline_mode=pl.Buffered(1))


def _mix_out_call(a_act, b_act, gates, x, wa, wb, wo, gpost, gnext, *, tm=256):
    n, d = x.shape
    row = lambda: pl.BlockSpec((tm, d), lambda i: (i, 0))
    vec = lambda: pl.BlockSpec((1, d), lambda i: (0, 0))
    wspec = lambda w: _resident(w.shape, lambda i: (0, 0))
    return pl.pallas_call(
        _mix_out_kernel,
        grid=(n // tm,),
        in_specs=[row(), row(), row(), pl.BlockSpec((tm, d), lambda i: (i, 1)), row(),
                  wspec(wa), wspec(wb), wspec(wo), vec(), vec()],
        out_specs=[row(), row()],
        out_shape=[jax.ShapeDtypeStruct((n, d), F32), jax.ShapeDtypeStruct((n, d), BF16)],
        scratch_shapes=[pltpu.VMEM((tm, d), F32)],
        compiler_params=_params(1),
        name="mix_out",
    )(a_act, b_act, gates, gates, x, wa, wb, wo, gpost, gnext)


def _ffn_out_call(act, x, wd, gpost, gnext, *, tm=256):
    n, d = x.shape
    f = act.shape[1]
    row = lambda: pl.BlockSpec((tm, d), lambda i: (i, 0))
    vec = lambda: pl.BlockSpec((1, d), lambda i: (0, 0))
    in_specs = [pl.BlockSpec((tm, f), lambda i: (i, 0)), row(),
                _resident((f, d), lambda i: (0, 0)), vec()]
    args = [act, x, wd, gpost]
    out_specs = [row()]
    out_shape = [jax.ShapeDtypeStruct((n, d), F32)]
    if gnext is not None:
        in_specs.append(vec())
        args.append(gnext)
        out_specs.append(row())
        out_shape.append(jax.ShapeDtypeStruct((n, d), BF16))
    outs = pl.pallas_call(
        _ffn_out_kernel,
        grid=(n // tm,),
        in_specs=in_specs,
        out_specs=out_specs,
        out_shape=out_shape,
        scratch_shapes=[pltpu.VMEM((tm, d), F32)],
        compiler_params=_params(1),
        name="ffn_out",
    )(*args)
    return (outs[0], outs[1]) if gnext is not None else (outs[0], None)


def kernel(x, norm_mix_pre, norm_mix_post, norm_ffn_pre, norm_ffn_post, w_in, b_gate, conv_w,
           conv_b, conv_ln_g, conv_ln_b, w_a_out, sgu_ln_g, sgu_ln_b, w_spatial, b_spatial,
           w_b_out, w_o, w_gate_up, w_down):
    bsz, seq, d = x.shape
    depth = w_in.shape[0]
    n = bsz * seq
    ffn = w_down.shape[1]
    gc = d // SGU_GROUPS

    xf = x.reshape(n, d)
    h = _rms_call(xf, norm_mix_pre[0][None])
    for l in range(depth):
        glu, w_a_b = _proj_call(h, w_in, l, (0, d), d, _glu_epilogue, w_a_out, tm=2048, tn=512,
                                name="proj_glu")
        z, w_b_b = _proj_call(h, w_in, l, (2 * d,), 2 * d, _gelu_epilogue, w_b_out, tm=1024, tn=1024,
                              name="proj_gelu")
        gates, w_o_b = _proj_call(h, w_in, l, (4 * d,), 2 * d, _gate_epilogue, w_o,
                                  bias=b_gate[l].reshape(1, 2 * d), tm=1024, tn=1024,
                                  name="proj_gate")
        bsp = jnp.broadcast_to(b_spatial[l][:, :, None], (SGU_GROUPS, CHUNK, gc))
        a_act, b_act = _branch_act_call(
            glu.reshape(bsz, seq, d), conv_w[l], conv_b[l][None], conv_ln_g[l][None],
            conv_ln_b[l][None], z.reshape(bsz, seq, 2 * d), sgu_ln_g[l][None], sgu_ln_b[l][None],
            w_spatial[l], bsp)
        a_act, b_act = a_act.reshape(n, d), b_act.reshape(n, d)
        xf, h = _mix_out_call(a_act, b_act, gates, xf, w_a_b, w_b_b, w_o_b,
                              norm_mix_post[l][None], norm_ffn_pre[l][None])
        act, w_d_b = _proj_call(h, w_gate_up, l, (0, ffn), ffn, _swiglu_epilogue, w_down,
                                tm=2048, tn=512, name="proj_swiglu")
        gnext = norm_mix_pre[l + 1][None] if l + 1 < depth else None
        xf, h = _ffn_out_call(act, xf, w_d_b, norm_ffn_post[l][None], gnext)
    return xf.reshape(bsz, seq, d)
```
